```python
import jax, jax.numpy as jnp
from jax import lax
import numpy as np

D_MODEL = 1024
BATCH = 32
SEQ = 2048
DEPTH = 2

CTX_LEN = 256
GRID_W = 64
EPS = 1e-6

HEAD_DIM = 64
ATT_WIDTH = D_MODEL // 2
N_HEADS = ATT_WIDTH // HEAD_DIM
N_KV_HEADS = N_HEADS // 4
KV_WIDTH = N_KV_HEADS * HEAD_DIM
Q_BLOCK = 128
ROPE_BASE = 10000.0

CHUNK = 128
SGU_WIDTH = D_MODEL // 4
N_SGU_GROUPS = 4
SGU_GROUP_DIM = SGU_WIDTH // N_SGU_GROUPS

POOL_WINDOWS = (2, 4, 8, 16)
POOL_WIDTH = D_MODEL // 4
N_POOL_GROUPS = len(POOL_WINDOWS)
POOL_GROUP_DIM = POOL_WIDTH // N_POOL_GROUPS

MIX_WIDTH = ATT_WIDTH + SGU_WIDTH + POOL_WIDTH
Q_END = ATT_WIDTH
K_END = Q_END + KV_WIDTH
V_END = K_END + KV_WIDTH
U_END = V_END + SGU_WIDTH
SV_END = U_END + SGU_WIDTH
IN_WIDTH = SV_END + POOL_WIDTH

D_FF = 2816
CONV_WIDTH = 3

kernel_name = "hybrid_parallel_attn_sgu_pool_convffn"


def rmsnorm(x, g):
    xf = x.astype(jnp.float32)
    y = xf * lax.rsqrt(jnp.mean(xf * xf, axis=-1, keepdims=True) + EPS)
    return (y * g.astype(jnp.float32)).astype(x.dtype)


def modulate(h, shift, scale):
    return h * (1 + scale) + shift


def axial_rope_tables(n, dtype):
    rows = n // GRID_W
    row = jnp.repeat(jnp.arange(rows), GRID_W).astype(jnp.float32)
    col = jnp.tile(jnp.arange(GRID_W), rows).astype(jnp.float32)
    axis_dim = HEAD_DIM // 2
    inv = ROPE_BASE ** (-jnp.arange(0, axis_dim, 2, dtype=jnp.float32) / axis_dim)
    ang_r = row[:, None] * inv[None, :]
    ang_c = col[:, None] * inv[None, :]
    return (jnp.cos(ang_r).astype(dtype), jnp.sin(ang_r).astype(dtype),
            jnp.cos(ang_c).astype(dtype), jnp.sin(ang_c).astype(dtype))


def _rotate(xp, cos, sin):
    half = xp.shape[-1] // 2
    x1, x2 = xp[..., :half], xp[..., half:]
    c = cos[None, :, None, :]
    s = sin[None, :, None, :]
    return jnp.concatenate([x1 * c - x2 * s, x2 * c + x1 * s], axis=-1)


def apply_axial_rope(x, tabs):
    cos_r, sin_r, cos_c, sin_c = tabs
    axis_dim = HEAD_DIM // 2
    return jnp.concatenate([_rotate(x[..., :axis_dim], cos_r, sin_r),
                            _rotate(x[..., axis_dim:], cos_c, sin_c)], axis=-1)


def q_heads(p, q_gain):
    B, L, _ = p.shape
    q = p[..., :Q_END].reshape(B, L, N_HEADS, HEAD_DIM)
    return rmsnorm(q, q_gain)


def kv_heads(p, k_gain):
    B, L, _ = p.shape
    k = p[..., Q_END:K_END].reshape(B, L, N_KV_HEADS, HEAD_DIM)
    v = p[..., K_END:V_END].reshape(B, L, N_KV_HEADS, HEAD_DIM)
    return rmsnorm(k, k_gain), v


def _attend(qi, k, v):
    s = jnp.einsum('bqkgd,bnkd->bkgqn', qi, k, preferred_element_type=jnp.float32)
    pr = jax.nn.softmax(s, axis=-1).astype(v.dtype)
    return jnp.einsum('bkgqn,bnkd->bqkgd', pr, v)


def latent_attention(q, kx, vx, kc, vc):
    B, S, H, hd = q.shape
    G = H // N_KV_HEADS
    k = jnp.concatenate([kx, kc], axis=1)
    v = jnp.concatenate([vx, vc], axis=1)
    q = q * (hd ** -0.5)
    qb = q.reshape(B, S // Q_BLOCK, Q_BLOCK, N_KV_HEADS, G, hd).transpose(1, 0, 2, 3, 4, 5)
    o = lax.map(lambda qi: _attend(qi, k, v), qb)
    return o.transpose(1, 0, 2, 3, 4, 5).reshape(B, S, H * hd)


def context_attention(q, k, v):
    B, C, H, hd = q.shape
    G = H // N_KV_HEADS
    qg = (q * (hd ** -0.5)).reshape(B, C, N_KV_HEADS, G, hd)
    return _attend(qg, k, v).reshape(B, C, H * hd)


def spatial_gating(p, w_s, b_s):
    B, L, _ = p.shape
    u = p[..., V_END:U_END]
    v = p[..., U_END:SV_END].reshape(B, L // CHUNK, CHUNK, N_SGU_GROUPS, SGU_GROUP_DIM)
    mix = jnp.einsum('gpq,bnqgd->bnpgd', w_s, v) + b_s.T[None, None, :, :, None]
    return u * mix.reshape(B, L, SGU_WIDTH)


def multi_scale_pool(p, w_pool, pool_scale):
    B, L, _ = p.shape
    xp = p[..., SV_END:IN_WIDTH]
    outs = []
    for gi, w in enumerate(POOL_WINDOWS):
        xg = xp[..., gi * POOL_GROUP_DIM:(gi + 1) * POOL_GROUP_DIM].astype(jnp.float32)
        left = w // 2
        right = w - 1 - left
        cs = jnp.cumsum(jnp.pad(xg, ((0, 0), (left + 1, right), (0, 0))), axis=1)
        tot = cs[:, w:w + L] - cs[:, :L]
        t = np.arange(L)
        cnt = (np.minimum(t + right, L - 1) - np.maximum(t - left, 0) + 1).astype(np.float32)
        mean = tot / jnp.asarray(cnt)[None, :, None]
        outs.append((mean - xg).astype(p.dtype))
    y = jnp.stack(outs, axis=2)
    y = jnp.einsum('blgd,gde->blge', y, w_pool).reshape(B, L, POOL_WIDTH)
    return y * pool_scale


def conv_ffn(h, w_up, conv_w, conv_b, w_down):
    z = h @ w_up
    zp = jnp.pad(z, ((0, 0), (1, 1), (0, 0)))
    z = zp[:, :-2] * conv_w[0] + zp[:, 1:-1] * conv_w[1] + zp[:, 2:] * conv_w[2] + conv_b
    g, val = z[..., :D_FF], z[..., D_FF:]
    return (jax.nn.silu(g) * val) @ w_down


def setup_inputs(seed: int = 0) -> dict:
    key = jax.random.key(seed)
    ks = jax.random.split(key, 24)
    f32 = jnp.float32
    nrm = lambda k, shape, s: jax.random.normal(k, shape, f32) * s
    D = D_MODEL
    return {
        "x": nrm(ks[0], (BATCH, SEQ, D), 1.0),
        "c": nrm(ks[1], (BATCH, D), 1.0),
        "ctx": nrm(ks[2], (BATCH, CTX_LEN, D), 1.0),
        "c_ctx": nrm(ks[3], (D,), 1.0),
        "w_mod": nrm(ks[4], (DEPTH, D, 6 * D), 0.5 * D ** -0.5),
        "b_mod": nrm(ks[5], (DEPTH, 6 * D), 0.01),
        "norm1_g": 1.0 + nrm(ks[6], (DEPTH, D), 0.02),
        "w_in": nrm(ks[7], (DEPTH, D, IN_WIDTH), D ** -0.5),
        "q_gain": 1.0 + nrm(ks[8], (DEPTH, HEAD_DIM), 0.02),
        "k_gain": 1.0 + nrm(ks[9], (DEPTH, HEAD_DIM), 0.02),
        "w_s": nrm(ks[10], (DEPTH, N_SGU_GROUPS, CHUNK, CHUNK), CHUNK ** -0.5),
        "b_s": nrm(ks[11], (DEPTH, N_SGU_GROUPS, CHUNK), 0.01),
        "w_pool": nrm(ks[12], (DEPTH, N_POOL_GROUPS, POOL_GROUP_DIM, POOL_GROUP_DIM), POOL_GROUP_DIM ** -0.5),
        "pool_scale": 1.0 + nrm(ks[13], (DEPTH, POOL_WIDTH), 0.1),
        "w_out": nrm(ks[14], (DEPTH, MIX_WIDTH, D), MIX_WIDTH ** -0.5),
        "norm2_g": 1.0 + nrm(ks[15], (DEPTH, D), 0.02),
        "w_up": nrm(ks[16], (DEPTH, D, 2 * D_FF), D ** -0.5),
        "conv_w": nrm(ks[17], (DEPTH, CONV_WIDTH, 2 * D_FF), CONV_WIDTH ** -0.5),
        "conv_b": nrm(ks[18], (DEPTH, 2 * D_FF), 0.01),
        "w_down": nrm(ks[19], (DEPTH, D_FF, D), D_FF ** -0.5),
        "final_g": 1.0 + nrm(ks[20], (D,), 0.02),
    }


def reference(x, c, ctx, c_ctx, w_mod, b_mod, norm1_g, w_in, q_gain, k_gain, w_s, b_s,
              w_pool, pool_scale, w_out, norm2_g, w_up, conv_w, conv_b, w_down, final_g):
    tabs = axial_rope_tables(x.shape[1], x.dtype)
    silu_c = jax.nn.silu(c)
    silu_cc = jax.nn.silu(c_ctx)
    for i in range(DEPTH):
        last = i == DEPTH - 1
        mod_x = (silu_c @ w_mod[i] + b_mod[i])[:, None, :]
        mod_c = silu_cc @ w_mod[i] + b_mod[i]
        sh1, sc1, g1, sh2, sc2, g2 = jnp.split(mod_x, 6, axis=-1)
        csh1, csc1, cg1, csh2, csc2, cg2 = jnp.split(mod_c, 6, axis=-1)

        hx = modulate(rmsnorm(x, norm1_g[i]), sh1, sc1)
        hc = modulate(rmsnorm(ctx, norm1_g[i]), csh1, csc1)
        px = hx @ w_in[i]
        pc = hc @ w_in[i]

        qx = apply_axial_rope(q_heads(px, q_gain[i]), tabs)
        kx, vx = kv_heads(px, k_gain[i])
        kx = apply_axial_rope(kx, tabs)
        kc, vc = kv_heads(pc, k_gain[i])

        att_x = latent_attention(qx, kx, vx, kc, vc)
        sgu_x = spatial_gating(px, w_s[i], b_s[i])
        pool_x = multi_scale_pool(px, w_pool[i], pool_scale[i])
        mix_x = jnp.concatenate([att_x, sgu_x, pool_x], axis=-1) @ w_out[i]
        x = x + g1 * mix_x

        x = x + g2 * conv_ffn(modulate(rmsnorm(x, norm2_g[i]), sh2, sc2),
                              w_up[i], conv_w[i], conv_b[i], w_down[i])

        if not last:
            qc = q_heads(pc, q_gain[i])
            att_c = context_attention(qc, kc, vc)
            sgu_c = spatial_gating(pc, w_s[i], b_s[i])
            pool_c = multi_scale_pool(pc, w_pool[i], pool_scale[i])
            mix_c = jnp.concatenate([att_c, sgu_c, pool_c], axis=-1) @ w_out[i]
            ctx = ctx + cg1 * mix_c
            ctx = ctx + cg2 * conv_ffn(modulate(rmsnorm(ctx, norm2_g[i]), csh2, csc2),
                                       w_up[i], conv_w[i], conv_b[i], w_down[i])
    return rmsnorm(x, final_g)
```

```python
import functools
import math

import numpy as np
import jax
import jax.numpy as jnp
from jax import lax
from jax.experimental import pallas as pl
from jax.experimental.pallas import tpu as pltpu

F32 = jnp.float32
BF16 = jnp.bfloat16

EPS = 1e-6
GRID_W = 64
HEAD_DIM = 64
N_HEADS = 8
N_KV_HEADS = 2
ROPE_BASE = 10000.0
CHUNK = 128
N_SGU_GROUPS = 4
POOL_WINDOWS = (2, 4, 8, 16)
POOL_GROUP_DIM = 64
LANES = 128
SUBLANES = 8
HALO = SUBLANES
VMEM_LIMIT = 56 * 1024 * 1024


def _cparams(n_axes):
    return pltpu.CompilerParams(
        dimension_semantics=("arbitrary",) * n_axes, vmem_limit_bytes=VMEM_LIMIT)


def _const_spec(shape):
    nd = len(shape)
    return pl.BlockSpec(shape, lambda *_: (0,) * nd, pipeline_mode=pl.Buffered(1))


def _silu(x):
    return x * (1.0 / (1.0 + jnp.exp(-x)))


def _rms_mod(x, g, shift, scale):
    ms = jnp.mean(x * x, axis=-1, keepdims=True)
    y = x * lax.rsqrt(ms + EPS) * g
    return y * (1.0 + scale) + shift


def _mod_kernel(c_ref, w_ref, b_ref, o_ref):
    s = _silu(c_ref[...]).astype(BF16)
    o_ref[0] = jnp.dot(s, w_ref[0].astype(BF16), preferred_element_type=F32) + b_ref[0]


def _mod_call(cvec, w_mod, b_mod):
    depth, d, n = w_mod.shape
    rows = cvec.shape[0]
    tn = 1536
    return pl.pallas_call(
        _mod_kernel,
        grid=(depth, n // tn),
        in_specs=[
            pl.BlockSpec((rows, d), lambda i, j: (0, 0)),
            pl.BlockSpec((1, d, tn), lambda i, j: (i, 0, j)),
            pl.BlockSpec((1, 1, tn), lambda i, j: (i, 0, j)),
        ],
        out_specs=pl.BlockSpec((1, rows, tn), lambda i, j: (i, 0, j)),
        out_shape=jax.ShapeDtypeStruct((depth, rows, n), F32),
        compiler_params=_cparams(2),
        name="mod",
    )(cvec, w_mod, b_mod.reshape(depth, 1, n))


def _proj_kernel(x_ref, sh_ref, sc_ref, g_ref, w_ref, e_ref, gain_ref, cos_ref, sin_ref,
                 q_ref, k_ref, v_ref, u_ref, sv_ref, xp_ref):
    t = x_ref.shape[1]
    h = _rms_mod(x_ref[0], g_ref[...], sh_ref[0], sc_ref[0]).astype(BF16)
    p = jnp.dot(h, w_ref[...], preferred_element_type=F32)
    nqk = (N_HEADS + N_KV_HEADS) * HEAD_DIM
    qk = p[:, :nqk]
    msh = jnp.dot((qk * qk).astype(BF16), e_ref[...], preferred_element_type=F32)
    qkn = qk * lax.rsqrt(msh + EPS) * gain_ref[...]
    c = cos_ref[...]
    s = sin_ref[...]
    lane = lax.broadcasted_iota(jnp.int32, (t, LANES), 1)
    first = (lane % 32) < 16
    lo = lane < HEAD_DIM
    for j in range(nqk // LANES):
        blk = qkn[:, LANES * j:LANES * (j + 1)]
        sw = jnp.where(first, pltpu.roll(blk, LANES - 16, axis=1), pltpu.roll(blk, 16, axis=1))
        r = blk * c + sw * s
        if j < N_HEADS // 2:
            kvh = (2 * j) // (N_HEADS // N_KV_HEADS)
            rr = pltpu.roll(r, HEAD_DIM, axis=1)
            keep = lo if kvh == 0 else jnp.logical_not(lo)
            even = r if kvh == 0 else rr
            odd = rr if kvh == 0 else r
            q_ref[0, :, LANES * (2 * j):LANES * (2 * j + 1)] = jnp.where(keep, even, 0.0).astype(BF16)
            q_ref[0, :, LANES * (2 * j + 1):LANES * (2 * j + 2)] = jnp.where(keep, odd, 0.0).astype(BF16)
        else:
            k_ref[0] = r.astype(BF16)
    v_ref[0] = p[:, 640:768].astype(BF16)
    u_ref[0] = p[:, 768:1024]
    sv_ref[0] = p[:, 1024:1280].astype(BF16)
    xp_ref[0] = p[:, 1280:1536]


def _proj_call(x, sh, sc, g, w_in, e_mat, gain, cos_t, sin_t, tile):
    b, l, d = x.shape
    nt = l // tile
    n_in = w_in.shape[1]
    row = lambda t_, b_: (b_, t_, 0)
    vec = lambda t_, b_: (b_, 0, 0)
    tab = lambda t_, b_: (t_, 0)
    return pl.pallas_call(
        _proj_kernel,
        grid=(nt, b),
        in_specs=[
            pl.BlockSpec((1, tile, d), row),
            pl.BlockSpec((1, 1, d), vec),
            pl.BlockSpec((1, 1, d), vec),
            _const_spec((1, d)),
            _const_spec((d, n_in)),
            _const_spec(e_mat.shape),
            _const_spec(gain.shape),
            pl.BlockSpec((tile, LANES), tab),
            pl.BlockSpec((tile, LANES), tab),
        ],
        out_specs=[
            pl.BlockSpec((1, tile, N_HEADS * LANES), row),
            pl.BlockSpec((1, tile, LANES), row),
            pl.BlockSpec((1, tile, LANES), row),
            pl.BlockSpec((1, tile, 256), row),
            pl.BlockSpec((1, tile, 256), row),
            pl.BlockSpec((1, tile, 256), row),
        ],
        out_shape=[
            jax.ShapeDtypeStruct((b, l, N_HEADS * LANES), BF16),
            jax.ShapeDtypeStruct((b, l, LANES), BF16),
            jax.ShapeDtypeStruct((b, l, LANES), BF16),
            jax.ShapeDtypeStruct((b, l, 256), F32),
            jax.ShapeDtypeStruct((b, l, 256), BF16),
            jax.ShapeDtypeStruct((b, l, 256), F32),
        ],
        compiler_params=_cparams(2),
        name="proj",
    )(x, sh, sc, g, w_in, e_mat, gain, cos_t, sin_t)


def _attn_kernel(q_ref, k_ref, vt_ref, o_ref):
    k = k_ref[0]
    vt = vt_ref[0]
    group = N_HEADS // N_KV_HEADS
    outs = []
    for i in range(N_HEADS):
        kvh = i // group
        qi = q_ref[0, :, LANES * i:LANES * (i + 1)]
        st = lax.dot_general(k, qi, (((1,), (1,)), ((), ())),
                             preferred_element_type=F32)
        m = jnp.max(st, axis=0, keepdims=True)
        p = jnp.exp(st - m)
        l = jnp.sum(p, axis=0, keepdims=True)
        ot = jnp.dot(vt, p.astype(BF16), preferred_element_type=F32)
        outs.append(ot[HEAD_DIM * kvh:HEAD_DIM * (kvh + 1)] * (1.0 / l))
    for j in range(N_HEADS // 2):
        pair = jnp.concatenate([outs[2 * j], outs[2 * j + 1]], axis=0)
        o_ref[0, :, LANES * j:LANES * (j + 1)] = pair.T.astype(BF16)


def _attn_call(q, k, vt, tq):
    b, s, _ = q.shape
    nk = k.shape[1]
    return pl.pallas_call(
        _attn_kernel,
        grid=(b, s // tq),
        in_specs=[
            pl.BlockSpec((1, tq, N_HEADS * LANES), lambda b_, t_: (b_, t_, 0)),
            pl.BlockSpec((1, nk, LANES), lambda b_, t_: (b_, 0, 0)),
            pl.BlockSpec((1, LANES, nk), lambda b_, t_: (b_, 0, 0)),
        ],
        out_specs=pl.BlockSpec((1, tq, N_HEADS * HEAD_DIM), lambda b_, t_: (b_, t_, 0)),
        out_shape=jax.ShapeDtypeStruct((b, s, N_HEADS * HEAD_DIM), BF16),
        compiler_params=_cparams(2),
        name="attn",
    )(q, k, vt)


def _mix_kernel(att_ref, u_ref, sv_ref, xp_ref, xpp_ref, xpn_ref, x_ref, g1_ref,
                ws_ref, bias_ref, wp_ref, ps_ref, icnt_ref, wo_ref, o_ref):
    t = x_ref.shape[1]
    ti = pl.program_id(1)
    nt = pl.num_programs(1)
    lane = lax.broadcasted_iota(jnp.int32, (CHUNK, LANES), 1)
    lo = lane < 64

    sv = sv_ref[0]
    rows = []
    for n in range(t // CHUNK):
        cols = []
        for jp in range(N_SGU_GROUPS // 2):
            blk = sv[CHUNK * n:CHUNK * (n + 1), LANES * jp:LANES * (jp + 1)]
            r0 = jnp.dot(ws_ref[2 * jp], blk, preferred_element_type=F32)
            r1 = jnp.dot(ws_ref[2 * jp + 1], blk, preferred_element_type=F32)
            cols.append(jnp.where(lo, r0, r1))
        rows.append(jnp.concatenate(cols, axis=1) + bias_ref[...])
    sgu = u_ref[0] * jnp.concatenate(rows, axis=0)

    xp = xp_ref[0]
    prev = jnp.where(ti > 0, xpp_ref[0], 0.0)
    nxt = jnp.where(ti < nt - 1, xpn_ref[0], 0.0)
    xe = jnp.concatenate([prev, xp, nxt], axis=0)
    lane_t = lax.broadcasted_iota(jnp.int32, (t, LANES), 1)
    lo_t = lane_t < 64

    def shifted(col, k_):
        return xe[HALO + k_:HALO + k_ + t, LANES * col:LANES * (col + 1)]

    def window(col, w, inner=None, inner_w=0):
        left = w // 2
        right = w - 1 - left
        ileft = inner_w // 2
        iright = inner_w - 1 - ileft
        acc = inner
        for k_ in range(-left, right + 1):
            if inner is not None and -ileft <= k_ <= iright:
                continue
            term = shifted(col, k_)
            acc = term if acc is None else acc + term
        return acc

    w2 = window(0, 2)
    w4 = window(0, 4, w2, 2)
    w8 = window(1, 8)
    w16 = window(1, 16, w8, 8)
    tot = jnp.concatenate([jnp.where(lo_t, w2, w4), jnp.where(lo_t, w8, w16)], axis=1)
    y = tot * icnt_ref[...] - xp
    pool = jnp.dot(y.astype(BF16), wp_ref[...], preferred_element_type=F32) * ps_ref[...]

    na = att_ref.shape[2]
    ns = sgu.shape[1]
    mix = jnp.dot(att_ref[0], wo_ref[0:na, :], preferred_element_type=F32)
    mix = mix + jnp.dot(sgu.astype(BF16), wo_ref[na:na + ns, :], preferred_element_type=F32)
    mix = mix + jnp.dot(pool.astype(BF16), wo_ref[na + ns:, :], preferred_element_type=F32)
    o_ref[0] = x_ref[0] + g1_ref[0] * mix


def _mix_call(att, u, sv, xp, x, g1, ws, bias, wp_bd, pscale, icnt, w_out, tile):
    b, l, d = x.shape
    nt = l // tile
    hb = tile // HALO
    nhb = l // HALO
    row = lambda b_, t_: (b_, t_, 0)
    vec = lambda b_, t_: (b_, 0, 0)
    prev = lambda b_, t_: (b_, jnp.maximum(t_ * hb - 1, 0), 0)
    nxt = lambda b_, t_: (b_, jnp.minimum((t_ + 1) * hb, nhb - 1), 0)
    return pl.pallas_call(
        _mix_kernel,
        grid=(b, nt),
        in_specs=[
            pl.BlockSpec((1, tile, att.shape[2]), row),
            pl.BlockSpec((1, tile, 256), row),
            pl.BlockSpec((1, tile, 256), row),
            pl.BlockSpec((1, tile, 256), row),
            pl.BlockSpec((1, HALO, 256), prev),
            pl.BlockSpec((1, HALO, 256), nxt),
            pl.BlockSpec((1, tile, d), row),
            pl.BlockSpec((1, 1, d), vec),
            _const_spec(ws.shape),
            _const_spec(bias.shape),
            _const_spec(wp_bd.shape),
            _const_spec(pscale.shape),
            pl.BlockSpec((tile, 256), lambda b_, t_: (t_, 0)),
            _const_spec(w_out.shape),
        ],
        out_specs=pl.BlockSpec((1, tile, d), row),
        out_shape=jax.ShapeDtypeStruct((b, l, d), F32),
        compiler_params=_cparams(2),
        name="mix",
    )(att, u, sv, xp, xp, xp, x, g1, ws, bias, wp_bd, pscale, icnt, w_out)


def _ffn_kernel(x_ref, xp_ref, xn_ref, sh_ref, sc_ref, g2_ref, ng_ref, wg_ref, wv_ref,
                cwg_ref, cwv_ref, cbg_ref, cbv_ref, wd_ref, fg_ref, o_ref, *, final):
    t = x_ref.shape[1]
    ti = pl.program_id(1)
    nt = pl.num_programs(1)
    x = x_ref[0]
    ng = ng_ref[...]
    sh = sh_ref[0]
    sc = sc_ref[0]
    h = _rms_mod(x, ng, sh, sc).astype(BF16)
    xh = jnp.concatenate([xp_ref[0], xn_ref[0]], axis=0)
    hh = _rms_mod(xh, ng, sh, sc).astype(BF16)
    mprev = jnp.where(ti > 0, 1.0, 0.0)
    mnext = jnp.where(ti < nt - 1, 1.0, 0.0)
    fc = wg_ref.shape[2]
    rowi = lax.broadcasted_iota(jnp.int32, (t, fc), 0)
    is_first = rowi == 0
    is_last = rowi == t - 1

    def conv(z, zh, cw, cb):
        zprev = zh[HALO - 1:HALO] * mprev
        znext = zh[HALO:HALO + 1] * mnext
        zdn = jnp.where(is_first, zprev, pltpu.roll(z, 1, axis=0))
        zup = jnp.where(is_last, znext, pltpu.roll(z, t - 1, axis=0))
        return zdn * cw[0:1] + z * cw[1:2] + zup * cw[2:3] + cb

    acc = jnp.zeros((t, x.shape[1]), F32)
    for c in range(wg_ref.shape[0]):
        zg = jnp.dot(h, wg_ref[c], preferred_element_type=F32)
        zgh = jnp.dot(hh, wg_ref[c], preferred_element_type=F32)
        zv = jnp.dot(h, wv_ref[c], preferred_element_type=F32)
        zvh = jnp.dot(hh, wv_ref[c], preferred_element_type=F32)
        cg = conv(zg, zgh, cwg_ref[c], cbg_ref[c])
        cv = conv(zv, zvh, cwv_ref[c], cbv_ref[c])
        a = (_silu(cg) * cv).astype(BF16)
        acc = acc + jnp.dot(a, wd_ref[c], preferred_element_type=F32)
    y = x + g2_ref[0] * acc
    if final:
        ms = jnp.mean(y * y, axis=-1, keepdims=True)
        y = y * lax.rsqrt(ms + EPS) * fg_ref[...]
    o_ref[0] = y


def _ffn_call(x, sh, sc, g2, ng, wg, wv, cwg, cwv, cbg, cbv, wd, fg, tile, final):
    b, l, d = x.shape
    nt = l // tile
    hb = tile // HALO
    nhb = l // HALO
    row = lambda b_, t_: (b_, t_, 0)
    vec = lambda b_, t_: (b_, 0, 0)
    prev = lambda b_, t_: (b_, jnp.maximum(t_ * hb - 1, 0), 0)
    nxt = lambda b_, t_: (b_, jnp.minimum((t_ + 1) * hb, nhb - 1), 0)
    return pl.pallas_call(
        functools.partial(_ffn_kernel, final=final),
        grid=(b, nt),
        in_specs=[
            pl.BlockSpec((1, tile, d), row),
            pl.BlockSpec((1, HALO, d), prev),
            pl.BlockSpec((1, HALO, d), nxt),
            pl.BlockSpec((1, 1, d), vec),
            pl.BlockSpec((1, 1, d), vec),
            pl.BlockSpec((1, 1, d), vec),
            _const_spec(ng.shape),
            _const_spec(wg.shape),
            _const_spec(wv.shape),
            _const_spec(cwg.shape),
            _const_spec(cwv.shape),
            _const_spec(cbg.shape),
            _const_spec(cbv.shape),
            _const_spec(wd.shape),
            _const_spec(fg.shape),
        ],
        out_specs=pl.BlockSpec((1, tile, d), row),
        out_shape=jax.ShapeDtypeStruct((b, l, d), F32),
        compiler_params=_cparams(2),
        name="ffn_final" if final else "ffn",
    )(x, x, x, sh, sc, g2, ng, wg, wv, cwg, cwv, cbg, cbv, wd, fg)


def _rope_tables(n):
    pos = jnp.arange(n)
    row = (pos // GRID_W).astype(F32)
    col = (pos % GRID_W).astype(F32)
    axis_dim = HEAD_DIM // 2
    inv = ROPE_BASE ** (-jnp.arange(0, axis_dim, 2, dtype=F32) / axis_dim)
    ang_r = row[:, None] * inv[None, :]
    ang_c = col[:, None] * inv[None, :]
    cr, sr, cc, sn = jnp.cos(ang_r), jnp.sin(ang_r), jnp.cos(ang_c), jnp.sin(ang_c)
    cos_h = jnp.concatenate([cr, cr, cc, cc], axis=1)
    sin_h = jnp.concatenate([-sr, sr, -sn, sn], axis=1)
    return jnp.tile(cos_h, (1, 2)), jnp.tile(sin_h, (1, 2))


def _pool_inv_count(l):
    t = np.arange(l)
    cols = []
    for w in POOL_WINDOWS:
        left = w // 2
        right = w - 1 - left
        cnt = (np.minimum(t + right, l - 1) - np.maximum(t - left, 0) + 1).astype(np.float32)
        cols.append(np.repeat((np.float32(1.0) / cnt)[:, None], POOL_GROUP_DIM, axis=1))
    return jnp.asarray(np.concatenate(cols, axis=1))


def _block_diag(w):
    g, a, b = w.shape
    out = jnp.zeros((g * a, g * b), w.dtype)
    for i in range(g):
        out = out.at[i * a:(i + 1) * a, i * b:(i + 1) * b].set(w[i])
    return out


def _head_mean_matrix():
    n = (N_HEADS + N_KV_HEADS) * HEAD_DIM
    idx = np.arange(n) // HEAD_DIM
    return jnp.asarray((idx[:, None] == idx[None, :]).astype(np.float32) / HEAD_DIM, dtype=BF16)


def _split6(m):
    return [m[..., i * (m.shape[-1] // 6):(i + 1) * (m.shape[-1] // 6)] for i in range(6)]


def kernel(x, c, ctx, c_ctx, w_mod, b_mod, norm1_g, w_in, q_gain, k_gain, w_s, b_s, w_pool,
           pool_scale, w_out, norm2_g, w_up, conv_w, conv_b, w_down, final_g):
    b, s, d = x.shape
    cl = ctx.shape[1]
    depth = w_mod.shape[0]
    d_ff = w_down.shape[1]
    fc = 256
    nch = d_ff // fc

    rows = ((b + 1 + SUBLANES - 1) // SUBLANES) * SUBLANES
    cvec = jnp.zeros((rows, d), F32).at[:b].set(c).at[b].set(c_ctx)
    mod = _mod_call(cvec, w_mod, b_mod)

    cos_x, sin_x = _rope_tables(s)
    cos_c = jnp.ones((cl, LANES), F32)
    sin_c = jnp.zeros((cl, LANES), F32)
    e_mat = _head_mean_matrix()
    icnt_x = _pool_inv_count(s)
    icnt_c = _pool_inv_count(cl)
    fg = final_g.reshape(1, d)

    tile_x = 256
    tile_c = min(256, cl)

    for i in range(depth):
        last = i == depth - 1
        mx = [m.reshape(b, 1, d) for m in _split6(mod[i, :b])]
        mc = [jnp.broadcast_to(m.reshape(1, 1, d), (b, 1, d)) for m in _split6(mod[i, b])]
        g1n = norm1_g[i].reshape(1, d)
        g2n = norm2_g[i].reshape(1, d)
        w_in_b = w_in[i].astype(BF16)
        w_out_b = w_out[i].astype(BF16)
        gain = jnp.concatenate([jnp.tile(q_gain[i] * (HEAD_DIM ** -0.5), N_HEADS),
                                jnp.tile(k_gain[i], N_KV_HEADS)]).reshape(1, -1)
        ws_b = w_s[i].astype(BF16)
        bias = jnp.repeat(b_s[i].T, d // 4 // N_SGU_GROUPS, axis=1)
        wp_bd = _block_diag(w_pool[i]).astype(BF16)
        pscale = pool_scale[i].reshape(1, -1)
        wup = w_up[i].astype(BF16)
        wg = wup[:, :d_ff].reshape(d, nch, fc).transpose(1, 0, 2)
        wv = wup[:, d_ff:].reshape(d, nch, fc).transpose(1, 0, 2)
        cwg = conv_w[i][:, :d_ff].reshape(3, nch, fc).transpose(1, 0, 2)
        cwv = conv_w[i][:, d_ff:].reshape(3, nch, fc).transpose(1, 0, 2)
        cbg = conv_b[i][:d_ff].reshape(nch, 1, fc)
        cbv = conv_b[i][d_ff:].reshape(nch, 1, fc)
        wd = w_down[i].astype(BF16).reshape(nch, fc, d)

        qx, kx, vx, ux, svx, xpx = _proj_call(x, mx[0], mx[1], g1n, w_in_b, e_mat, gain,
                                              cos_x, sin_x, tile_x)
        qc, kc, vc, uc, svc, xpc = _proj_call(ctx, mc[0], mc[1], g1n, w_in_b, e_mat, gain,
                                              cos_c, sin_c, tile_c)
        k_all = jnp.concatenate([kx, kc], axis=1)
        vt_all = jnp.swapaxes(jnp.concatenate([vx, vc], axis=1), 1, 2)
        att_x = _attn_call(qx, k_all, vt_all, 256)
        x = _mix_call(att_x, ux, svx, xpx, x, mx[2], ws_b, bias, wp_bd, pscale, icnt_x,
                      w_out_b, tile_x)
        x = _ffn_call(x, mx[3], mx[4], mx[5], g2n, wg, wv, cwg, cwv, cbg, cbv, wd, fg,
                      tile_x, last)
        if not last:
            att_c = _attn_call(qc, kc, jnp.swapaxes(vc, 1, 2), tile_c)
            ctx = _mix_call(att_c, uc, svc, xpc, ctx, mc[2], ws_b, bias, wp_bd, pscale, icnt_c,
                            w_out_b, tile_c)
            ctx = _ffn_call(ctx, mc[3], mc[4], mc[5], g2n, wg, wv, cwg, cwv, cbg, cbv, wd, fg,
                            tile_c, False)
    return x
```

```python
import functools
import math

import numpy as np
import jax
import jax.numpy as jnp
from jax import lax
from jax.experimental import pallas as pl
from jax.experimental.pallas import tpu as pltpu

F32 = jnp.float32
BF16 = jnp.bfloat16

EPS = 1e-6
GRID_W = 64
HEAD_DIM = 64
N_HEADS = 8
N_KV_HEADS = 2
ROPE_BASE = 10000.0
CHUNK = 128
N_SGU_GROUPS = 4
POOL_WINDOWS = (2, 4, 8, 16)
POOL_GROUP_DIM = 64
LANES = 128
SUBLANES = 8
HALO = SUBLANES
VMEM_LIMIT = 56 * 1024 * 1024
FFN_Z_SLOTS = 3
ATTN_SLOTS = 3


def _cparams(n_axes):
    return pltpu.CompilerParams(
        dimension_semantics=("arbitrary",) * n_axes, vmem_limit_bytes=VMEM_LIMIT)


def _const_spec(shape):
    nd = len(shape)
    return pl.BlockSpec(shape, lambda *_: (0,) * nd, pipeline_mode=pl.Buffered(1))


def _silu(x):
    return x * (1.0 / (1.0 + jnp.exp(-x)))


def _rms_mod(x, g, shift, scale):
    ms = jnp.mean(x * x, axis=-1, keepdims=True)
    y = x * lax.rsqrt(ms + EPS) * g
    return y * (1.0 + scale) + shift


def _mod_kernel(c_ref, w_ref, b_ref, o_ref):
    s = _silu(c_ref[...]).astype(BF16)
    o_ref[0] = jnp.dot(s, w_ref[0].astype(BF16), preferred_element_type=F32) + b_ref[0]


def _mod_call(cvec, w_mod, b_mod):
    depth, d, n = w_mod.shape
    rows = cvec.shape[0]
    tn = 1536
    return pl.pallas_call(
        _mod_kernel,
        grid=(depth, n // tn),
        in_specs=[
            pl.BlockSpec((rows, d), lambda i, j: (0, 0)),
            pl.BlockSpec((1, d, tn), lambda i, j: (i, 0, j)),
            pl.BlockSpec((1, 1, tn), lambda i, j: (i, 0, j)),
        ],
        out_specs=pl.BlockSpec((1, rows, tn), lambda i, j: (i, 0, j)),
        out_shape=jax.ShapeDtypeStruct((depth, rows, n), F32),
        compiler_params=_cparams(2),
        name="mod",
    )(cvec, w_mod, b_mod.reshape(depth, 1, n))


def _proj_kernel(x_ref, sh_ref, sc_ref, g_ref, w_ref, e_ref, gain_ref, cos_ref, sin_ref,
                 q_ref, k_ref, v_ref, u_ref, sv_ref, xp_ref):
    t = x_ref.shape[1]
    h = _rms_mod(x_ref[0], g_ref[...], sh_ref[0], sc_ref[0]).astype(BF16)
    p = jnp.dot(h, w_ref[...], preferred_element_type=F32)
    nqk = (N_HEADS + N_KV_HEADS) * HEAD_DIM
    qk = p[:, :nqk]
    msh = jnp.dot((qk * qk).astype(BF16), e_ref[...], preferred_element_type=F32)
    qkn = qk * lax.rsqrt(msh + EPS) * gain_ref[...]
    c = cos_ref[...]
    s = sin_ref[...]
    lane = lax.broadcasted_iota(jnp.int32, (t, LANES), 1)
    first = (lane % 32) < 16
    lo = lane < HEAD_DIM
    for j in range(nqk // LANES):
        blk = qkn[:, LANES * j:LANES * (j + 1)]
        sw = jnp.where(first, pltpu.roll(blk, LANES - 16, axis=1), pltpu.roll(blk, 16, axis=1))
        r = blk * c + sw * s
        if j < N_HEADS // 2:
            kvh = (2 * j) // (N_HEADS // N_KV_HEADS)
            rr = pltpu.roll(r, HEAD_DIM, axis=1)
            keep = lo if kvh == 0 else jnp.logical_not(lo)
            even = r if kvh == 0 else rr
            odd = rr if kvh == 0 else r
            q_ref[0, :, LANES * (2 * j):LANES * (2 * j + 1)] = jnp.where(keep, even, 0.0).astype(BF16)
            q_ref[0, :, LANES * (2 * j + 1):LANES * (2 * j + 2)] = jnp.where(keep, odd, 0.0).astype(BF16)
        else:
            k_ref[0] = r.astype(BF16)
    v_ref[0] = p[:, 640:768].astype(BF16)
    u_ref[0] = p[:, 768:1024]
    sv_ref[0] = p[:, 1024:1280].astype(BF16)
    xp_ref[0] = p[:, 1280:1536]


def _proj_call(x, sh, sc, g, w_in, e_mat, gain, cos_t, sin_t, tile):
    b, l, d = x.shape
    nt = l // tile
    n_in = w_in.shape[1]
    row = lambda t_, b_: (b_, t_, 0)
    vec = lambda t_, b_: (b_, 0, 0)
    tab = lambda t_, b_: (t_, 0)
    return pl.pallas_call(
        _proj_kernel,
        grid=(nt, b),
        in_specs=[
            pl.BlockSpec((1, tile, d), row),
            pl.BlockSpec((1, 1, d), vec),
            pl.BlockSpec((1, 1, d), vec),
            _const_spec((1, d)),
            _const_spec((d, n_in)),
            _const_spec(e_mat.shape),
            _const_spec(gain.shape),
            pl.BlockSpec((tile, LANES), tab),
            pl.BlockSpec((tile, LANES), tab),
        ],
        out_specs=[
            pl.BlockSpec((1, tile, N_HEADS * LANES), row),
            pl.BlockSpec((1, tile, LANES), row),
            pl.BlockSpec((1, tile, LANES), row),
            pl.BlockSpec((1, tile, 256), row),
            pl.BlockSpec((1, tile, 256), row),
            pl.BlockSpec((1, tile, 256), row),
        ],
        out_shape=[
            jax.ShapeDtypeStruct((b, l, N_HEADS * LANES), BF16),
            jax.ShapeDtypeStruct((b, l, LANES), BF16),
            jax.ShapeDtypeStruct((b, l, LANES), BF16),
            jax.ShapeDtypeStruct((b, l, 256), F32),
            jax.ShapeDtypeStruct((b, l, 256), BF16),
            jax.ShapeDtypeStruct((b, l, 256), F32),
        ],
        compiler_params=_cparams(2),
        name="proj",
    )(x, sh, sc, g, w_in, e_mat, gain, cos_t, sin_t)


def _attn_kernel(q_ref, k_ref, vt_ref, o_ref, s_ref, p_ref):
    k = k_ref[0]
    vt = vt_ref[0]
    group = N_HEADS // N_KV_HEADS
    nslot = s_ref.shape[0]

    def scores(i):
        qi = q_ref[0, :, LANES * i:LANES * (i + 1)]
        st = lax.dot_general(k, qi, (((1,), (1,)), ((), ())),
                             preferred_element_type=F32)
        s_ref[i % nslot] = st
        return jnp.max(st, axis=0, keepdims=True)

    def pv(i, l):
        kvh = i // group
        ot = jnp.dot(vt, p_ref[i % nslot], preferred_element_type=F32)
        return ot[HEAD_DIM * kvh:HEAD_DIM * (kvh + 1)] * (1.0 / l)

    outs = []
    m_next = scores(0)
    l_prev = None
    for i in range(N_HEADS):
        slot = i % nslot
        m = m_next
        if i + 1 < N_HEADS:
            m_next = scores(i + 1)
        p = jnp.exp2(s_ref[slot] - m)
        l = jnp.sum(p, axis=0, keepdims=True)
        p_ref[slot] = p.astype(BF16)
        if i >= 1:
            outs.append(pv(i - 1, l_prev))
        l_prev = l
    outs.append(pv(N_HEADS - 1, l_prev))
    for j in range(N_HEADS // 2):
        pair = jnp.concatenate([outs[2 * j], outs[2 * j + 1]], axis=0)
        o_ref[0, :, LANES * j:LANES * (j + 1)] = pair.T.astype(BF16)


def _attn_call(q, k, vt, tq):
    b, s, _ = q.shape
    nk = k.shape[1]
    return pl.pallas_call(
        _attn_kernel,
        grid=(b, s // tq),
        in_specs=[
            pl.BlockSpec((1, tq, N_HEADS * LANES), lambda b_, t_: (b_, t_, 0)),
            pl.BlockSpec((1, nk, LANES), lambda b_, t_: (b_, 0, 0)),
            pl.BlockSpec((1, LANES, nk), lambda b_, t_: (b_, 0, 0)),
        ],
        out_specs=pl.BlockSpec((1, tq, N_HEADS * HEAD_DIM), lambda b_, t_: (b_, t_, 0)),
        out_shape=jax.ShapeDtypeStruct((b, s, N_HEADS * HEAD_DIM), BF16),
        scratch_shapes=[pltpu.VMEM((ATTN_SLOTS, nk, tq), F32),
                        pltpu.VMEM((ATTN_SLOTS, nk, tq), BF16)],
        compiler_params=_cparams(2),
        name="attn",
    )(q, k, vt)


def _mix_kernel(att_ref, u_ref, sv_ref, xp_ref, xpp_ref, xpn_ref, x_ref, g1_ref,
                ws_ref, bias_ref, wp_ref, ps_ref, icnt_ref, wo_ref, o_ref):
    t = x_ref.shape[1]
    ti = pl.program_id(1)
    nt = pl.num_programs(1)
    lane = lax.broadcasted_iota(jnp.int32, (CHUNK, LANES), 1)
    lo = lane < 64

    sv = sv_ref[0]
    rows = []
    for n in range(t // CHUNK):
        cols = []
        for jp in range(N_SGU_GROUPS // 2):
            blk = sv[CHUNK * n:CHUNK * (n + 1), LANES * jp:LANES * (jp + 1)]
            r0 = jnp.dot(ws_ref[2 * jp], blk, preferred_element_type=F32)
            r1 = jnp.dot(ws_ref[2 * jp + 1], blk, preferred_element_type=F32)
            cols.append(jnp.where(lo, r0, r1))
        rows.append(jnp.concatenate(cols, axis=1) + bias_ref[...])
    sgu = u_ref[0] * jnp.concatenate(rows, axis=0)

    xp = xp_ref[0]
    prev = jnp.where(ti > 0, xpp_ref[0], 0.0)
    nxt = jnp.where(ti < nt - 1, xpn_ref[0], 0.0)
    xe = jnp.concatenate([prev, xp, nxt], axis=0)
    lane_t = lax.broadcasted_iota(jnp.int32, (t, LANES), 1)
    lo_t = lane_t < 64

    def shifted(col, k_):
        return xe[HALO + k_:HALO + k_ + t, LANES * col:LANES * (col + 1)]

    def window(col, w, inner=None, inner_w=0):
        left = w // 2
        right = w - 1 - left
        ileft = inner_w // 2
        iright = inner_w - 1 - ileft
        acc = inner
        for k_ in range(-left, right + 1):
            if inner is not None and -ileft <= k_ <= iright:
                continue
            term = shifted(col, k_)
            acc = term if acc is None else acc + term
        return acc

    w2 = window(0, 2)
    w4 = window(0, 4, w2, 2)
    w8 = window(1, 8)
    w16 = window(1, 16, w8, 8)
    tot = jnp.concatenate([jnp.where(lo_t, w2, w4), jnp.where(lo_t, w8, w16)], axis=1)
    y = tot * icnt_ref[...] - xp
    pool = jnp.dot(y.astype(BF16), wp_ref[...], preferred_element_type=F32) * ps_ref[...]

    na = att_ref.shape[2]
    ns = sgu.shape[1]
    mix = jnp.dot(att_ref[0], wo_ref[0:na, :], preferred_element_type=F32)
    mix = mix + jnp.dot(sgu.astype(BF16), wo_ref[na:na + ns, :], preferred_element_type=F32)
    mix = mix + jnp.dot(pool.astype(BF16), wo_ref[na + ns:, :], preferred_element_type=F32)
    o_ref[0] = x_ref[0] + g1_ref[0] * mix


def _mix_call(att, u, sv, xp, x, g1, ws, bias, wp_bd, pscale, icnt, w_out, tile):
    b, l, d = x.shape
    nt = l // tile
    hb = tile // HALO
    nhb = l // HALO
    row = lambda b_, t_: (b_, t_, 0)
    vec = lambda b_, t_: (b_, 0, 0)
    prev = lambda b_, t_: (b_, jnp.maximum(t_ * hb - 1, 0), 0)
    nxt = lambda b_, t_: (b_, jnp.minimum((t_ + 1) * hb, nhb - 1), 0)
    return pl.pallas_call(
        _mix_kernel,
        grid=(b, nt),
        in_specs=[
            pl.BlockSpec((1, tile, att.shape[2]), row),
            pl.BlockSpec((1, tile, 256), row),
            pl.BlockSpec((1, tile, 256), row),
            pl.BlockSpec((1, tile, 256), row),
            pl.BlockSpec((1, HALO, 256), prev),
            pl.BlockSpec((1, HALO, 256), nxt),
            pl.BlockSpec((1, tile, d), row),
            pl.BlockSpec((1, 1, d), vec),
            _const_spec(ws.shape),
            _const_spec(bias.shape),
            _const_spec(wp_bd.shape),
            _const_spec(pscale.shape),
            pl.BlockSpec((tile, 256), lambda b_, t_: (t_, 0)),
            _const_spec(w_out.shape),
        ],
        out_specs=pl.BlockSpec((1, tile, d), row),
        out_shape=jax.ShapeDtypeStruct((b, l, d), F32),
        compiler_params=_cparams(2),
        name="mix",
    )(att, u, sv, xp, xp, xp, x, g1, ws, bias, wp_bd, pscale, icnt, w_out)


def _ffn_kernel(x_ref, xp_ref, xn_ref, sh_ref, sc_ref, g2_ref, ng_ref, wg_ref, wv_ref,
                cwg_ref, cwv_ref, cbg_ref, cbv_ref, wd_ref, fg_ref, o_ref, z_ref, *, final):
    t = x_ref.shape[1]
    ti = pl.program_id(1)
    nt = pl.num_programs(1)
    x = x_ref[0]
    ng = ng_ref[...]
    sh = sh_ref[0]
    sc = sc_ref[0]
    hprev = jnp.where(ti > 0, _rms_mod(xp_ref[0], ng, sh, sc), 0.0)
    hnext = jnp.where(ti < nt - 1, _rms_mod(xn_ref[0], ng, sh, sc), 0.0)
    h = jnp.concatenate([hprev, _rms_mod(x, ng, sh, sc), hnext], axis=0).astype(BF16)
    nch = wg_ref.shape[0]

    nl = wg_ref.shape[2] // LANES
    nslot = z_ref.shape[0]

    def up(c):
        slot = c % nslot
        zg = jnp.dot(h, wg_ref[c], preferred_element_type=F32)
        zv = jnp.dot(h, wv_ref[c], preferred_element_type=F32)
        for j in range(nl):
            z_ref[slot, j] = zg[:, LANES * j:LANES * (j + 1)]
            z_ref[slot, nl + j] = zv[:, LANES * j:LANES * (j + 1)]

    def conv(slot, j, cw, cb):
        return (z_ref[slot, j, HALO - 1:HALO - 1 + t] * cw[0:1]
                + z_ref[slot, j, HALO:HALO + t] * cw[1:2]
                + z_ref[slot, j, HALO + 1:HALO + 1 + t] * cw[2:3] + cb)

    acc = jnp.zeros((t, x.shape[1]), F32)
    for c in range(min(nslot - 1, nch)):
        up(c)
    for c in range(nch):
        if c + nslot - 1 < nch:
            up(c + nslot - 1)
        slot = c % nslot
        parts = []
        for j in range(nl):
            ls = slice(LANES * j, LANES * (j + 1))
            cg = conv(slot, j, cwg_ref[c][:, ls], cbg_ref[c][:, ls])
            cv = conv(slot, nl + j, cwv_ref[c][:, ls], cbv_ref[c][:, ls])
            parts.append((_silu(cg) * cv).astype(BF16))
        a = jnp.concatenate(parts, axis=1)
        acc = acc + jnp.dot(a, wd_ref[c], preferred_element_type=F32)
    y = x + g2_ref[0] * acc
    if final:
        ms = jnp.mean(y * y, axis=-1, keepdims=True)
        y = y * lax.rsqrt(ms + EPS) * fg_ref[...]
    o_ref[0] = y


def _ffn_call(x, sh, sc, g2, ng, wg, wv, cwg, cwv, cbg, cbv, wd, fg, tile, final):
    b, l, d = x.shape
    nt = l // tile
    hb = tile // HALO
    nhb = l // HALO
    row = lambda b_, t_: (b_, t_, 0)
    vec = lambda b_, t_: (b_, 0, 0)
    prev = lambda b_, t_: (b_, jnp.maximum(t_ * hb - 1, 0), 0)
    nxt = lambda b_, t_: (b_, jnp.minimum((t_ + 1) * hb, nhb - 1), 0)
    return pl.pallas_call(
        functools.partial(_ffn_kernel, final=final),
        grid=(b, nt),
        in_specs=[
            pl.BlockSpec((1, tile, d), row),
            pl.BlockSpec((1, HALO, d), prev),
            pl.BlockSpec((1, HALO, d), nxt),
            pl.BlockSpec((1, 1, d), vec),
            pl.BlockSpec((1, 1, d), vec),
            pl.BlockSpec((1, 1, d), vec),
            _const_spec(ng.shape),
            _const_spec(wg.shape),
            _const_spec(wv.shape),
            _const_spec(cwg.shape),
            _const_spec(cwv.shape),
            _const_spec(cbg.shape),
            _const_spec(cbv.shape),
            _const_spec(wd.shape),
            _const_spec(fg.shape),
        ],
        out_specs=pl.BlockSpec((1, tile, d), row),
        out_shape=jax.ShapeDtypeStruct((b, l, d), F32),
        scratch_shapes=[pltpu.VMEM((FFN_Z_SLOTS, 2 * wg.shape[2] // LANES, tile + 2 * HALO, LANES), F32)],
        compiler_params=_cparams(2),
        name="ffn_final" if final else "ffn",
    )(x, x, x, sh, sc, g2, ng, wg, wv, cwg, cwv, cbg, cbv, wd, fg)


def _rope_tables(n):
    pos = jnp.arange(n)
    row = (pos // GRID_W).astype(F32)
    col = (pos % GRID_W).astype(F32)
    axis_dim = HEAD_DIM // 2
    inv = ROPE_BASE ** (-jnp.arange(0, axis_dim, 2, dtype=F32) / axis_dim)
    ang_r = row[:, None] * inv[None, :]
    ang_c = col[:, None] * inv[None, :]
    cr, sr, cc, sn = jnp.cos(ang_r), jnp.sin(ang_r), jnp.cos(ang_c), jnp.sin(ang_c)
    cos_h = jnp.concatenate([cr, cr, cc, cc], axis=1)
    sin_h = jnp.concatenate([-sr, sr, -sn, sn], axis=1)
    return jnp.tile(cos_h, (1, 2)), jnp.tile(sin_h, (1, 2))


def _pool_inv_count(l):
    t = np.arange(l)
    cols = []
    for w in POOL_WINDOWS:
        left = w // 2
        right = w - 1 - left
        cnt = (np.minimum(t + right, l - 1) - np.maximum(t - left, 0) + 1).astype(np.float32)
        cols.append(np.repeat((np.float32(1.0) / cnt)[:, None], POOL_GROUP_DIM, axis=1))
    return jnp.asarray(np.concatenate(cols, axis=1))


def _block_diag(w):
    g, a, b = w.shape
    out = jnp.zeros((g * a, g * b), w.dtype)
    for i in range(g):
        out = out.at[i * a:(i + 1) * a, i * b:(i + 1) * b].set(w[i])
    return out


def _head_mean_matrix():
    n = (N_HEADS + N_KV_HEADS) * HEAD_DIM
    idx = np.arange(n) // HEAD_DIM
    return jnp.asarray((idx[:, None] == idx[None, :]).astype(np.float32) / HEAD_DIM, dtype=BF16)


def _split6(m):
    return [m[..., i * (m.shape[-1] // 6):(i + 1) * (m.shape[-1] // 6)] for i in range(6)]


def kernel(x, c, ctx, c_ctx, w_mod, b_mod, norm1_g, w_in, q_gain, k_gain, w_s, b_s, w_pool,
           pool_scale, w_out, norm2_g, w_up, conv_w, conv_b, w_down, final_g):
    b, s, d = x.shape
    cl = ctx.shape[1]
    depth = w_mod.shape[0]
    d_ff = w_down.shape[1]
    fc = 256
    nch = d_ff // fc

    rows = ((b + 1 + SUBLANES - 1) // SUBLANES) * SUBLANES
    cvec = jnp.zeros((rows, d), F32).at[:b].set(c).at[b].set(c_ctx)
    mod = _mod_call(cvec, w_mod, b_mod)

    cos_x, sin_x = _rope_tables(s)
    cos_c = jnp.ones((cl, LANES), F32)
    sin_c = jnp.zeros((cl, LANES), F32)
    e_mat = _head_mean_matrix()
    icnt_x = _pool_inv_count(s)
    icnt_c = _pool_inv_count(cl)
    fg = final_g.reshape(1, d)

    tile_x = 256
    tile_c = min(256, cl)

    for i in range(depth):
        last = i == depth - 1
        mx = [m.reshape(b, 1, d) for m in _split6(mod[i, :b])]
        mc = [jnp.broadcast_to(m.reshape(1, 1, d), (b, 1, d)) for m in _split6(mod[i, b])]
        g1n = norm1_g[i].reshape(1, d)
        g2n = norm2_g[i].reshape(1, d)
        w_in_b = w_in[i].astype(BF16)
        w_out_b = w_out[i].astype(BF16)
        gain = jnp.concatenate([jnp.tile(q_gain[i] * (HEAD_DIM ** -0.5 * math.log2(math.e)), N_HEADS),
                                jnp.tile(k_gain[i], N_KV_HEADS)]).reshape(1, -1)
        ws_b = w_s[i].astype(BF16)
        bias = jnp.repeat(b_s[i].T, d // 4 // N_SGU_GROUPS, axis=1)
        wp_bd = _block_diag(w_pool[i]).astype(BF16)
        pscale = pool_scale[i].reshape(1, -1)
        wup = w_up[i].astype(BF16)
        wg = wup[:, :d_ff].reshape(d, nch, fc).transpose(1, 0, 2)
        wv = wup[:, d_ff:].reshape(d, nch, fc).transpose(1, 0, 2)
        cwg = conv_w[i][:, :d_ff].reshape(3, nch, fc).transpose(1, 0, 2)
        cwv = conv_w[i][:, d_ff:].reshape(3, nch, fc).transpose(1, 0, 2)
        cbg = conv_b[i][:d_ff].reshape(nch, 1, fc)
        cbv = conv_b[i][d_ff:].reshape(nch, 1, fc)
        wd = w_down[i].astype(BF16).reshape(nch, fc, d)

        qx, kx, vx, ux, svx, xpx = _proj_call(x, mx[0], mx[1], g1n, w_in_b, e_mat, gain,
                                              cos_x, sin_x, tile_x)
        qc, kc, vc, uc, svc, xpc = _proj_call(ctx, mc[0], mc[1], g1n, w_in_b, e_mat, gain,
                                              cos_c, sin_c, tile_c)
        k_all = jnp.concatenate([kx, kc], axis=1)
        vt_all = jnp.swapaxes(jnp.concatenate([vx, vc], axis=1), 1, 2)
        att_x = _attn_call(qx, k_all, vt_all, 256)
        x = _mix_call(att_x, ux, svx, xpx, x, mx[2], ws_b, bias, wp_bd, pscale, icnt_x,
                      w_out_b, tile_x)
        x = _ffn_call(x, mx[3], mx[4], mx[5], g2n, wg, wv, cwg, cwv, cbg, cbv, wd, fg,
                      tile_x, last)
        if not last:
            att_c = _attn_call(qc, kc, jnp.swapaxes(vc, 1, 2), tile_c)
            ctx = _mix_call(att_c, uc, svc, xpc, ctx, mc[2], ws_b, bias, wp_bd, pscale, icnt_c,
                            w_out_b, tile_c)
            ctx = _ffn_call(ctx, mc[3], mc[4], mc[5], g2n, wg, wv, cwg, cwv, cbg, cbv, wd, fg,
                            tile_c, False)
    return x
```

```python
import functools
import math

import numpy as np
import jax
import jax.numpy as jnp
from jax import lax
from jax.experimental import pallas as pl
from jax.experimental.pallas import tpu as pltpu

F32 = jnp.float32
BF16 = jnp.bfloat16

EPS = 1e-6
GRID_W = 64
HEAD_DIM = 64
N_HEADS = 8
N_KV_HEADS = 2
ROPE_BASE = 10000.0
CHUNK = 128
N_SGU_GROUPS = 4
POOL_WINDOWS = (2, 4, 8, 16)
POOL_GROUP_DIM = 64
LANES = 128
SUBLANES = 8
HALO = SUBLANES
VMEM_LIMIT = 56 * 1024 * 1024
FFN_Z_SLOTS = 3
ATTN_SLOTS = 2
ATTN_SUB = 256
ATTN_KEY_CHUNK = 512
PROJ_SUB = 256


def _cparams(n_axes):
    return pltpu.CompilerParams(
        dimension_semantics=("arbitrary",) * n_axes, vmem_limit_bytes=VMEM_LIMIT)


def _const_spec(shape):
    nd = len(shape)
    return pl.BlockSpec(shape, lambda *_: (0,) * nd, pipeline_mode=pl.Buffered(1))


def _silu(x):
    return x * (1.0 / (1.0 + jnp.exp(-x)))


def _rms_mod(x, g, shift, scale):
    ms = jnp.mean(x * x, axis=-1, keepdims=True)
    y = x * lax.rsqrt(ms + EPS) * g
    return y * (1.0 + scale) + shift


def _mod_kernel(c_ref, w_ref, b_ref, o_ref):
    s = _silu(c_ref[...]).astype(BF16)
    o_ref[0] = jnp.dot(s, w_ref[0].astype(BF16), preferred_element_type=F32) + b_ref[0]


def _mod_call(cvec, w_mod, b_mod):
    depth, d, n = w_mod.shape
    rows = cvec.shape[0]
    tn = 1536
    return pl.pallas_call(
        _mod_kernel,
        grid=(depth, n // tn),
        in_specs=[
            pl.BlockSpec((rows, d), lambda i, j: (0, 0)),
            pl.BlockSpec((1, d, tn), lambda i, j: (i, 0, j)),
            pl.BlockSpec((1, 1, tn), lambda i, j: (i, 0, j)),
        ],
        out_specs=pl.BlockSpec((1, rows, tn), lambda i, j: (i, 0, j)),
        out_shape=jax.ShapeDtypeStruct((depth, rows, n), F32),
        compiler_params=_cparams(2),
        name="mod",
    )(cvec, w_mod, b_mod.reshape(depth, 1, n))


def _proj_kernel(x_ref, sh_ref, sc_ref, g_ref, w_ref, eq_ref, ek_ref, gain_ref, cos_ref, sin_ref,
                 q_ref, k_ref, vt_ref, u_ref, sv_ref, xp_ref):
    t = x_ref.shape[1]
    sub = min(t, PROJ_SUB)
    nq = N_HEADS * HEAD_DIM
    nqk = (N_HEADS + N_KV_HEADS) * HEAD_DIM
    lane = lax.broadcasted_iota(jnp.int32, (sub, LANES), 1)
    first = (lane % 32) < 16
    lo = lane < HEAD_DIM

    def project(r):
        rows = slice(sub * r, sub * (r + 1))
        h = _rms_mod(x_ref[0, rows], g_ref[...], sh_ref[0], sc_ref[0]).astype(BF16)
        return jnp.dot(h, w_ref[...], preferred_element_type=F32)

    def finish(r, p):
        rows = slice(sub * r, sub * (r + 1))
        qk = p[:, :nqk]
        sq = (qk * qk).astype(BF16)
        msh = jnp.concatenate(
            [jnp.dot(sq[:, :nq], eq_ref[...], preferred_element_type=F32),
             jnp.dot(sq[:, nq:], ek_ref[...], preferred_element_type=F32)], axis=1)
        qkn = qk * lax.rsqrt(msh + EPS) * gain_ref[...]
        c = cos_ref[rows]
        s = sin_ref[rows]
        for j in range(nqk // LANES):
            blk = qkn[:, LANES * j:LANES * (j + 1)]
            sw = jnp.where(first, pltpu.roll(blk, LANES - 16, axis=1), pltpu.roll(blk, 16, axis=1))
            rot = blk * c + sw * s
            if j < N_HEADS // 2:
                kvh = (2 * j) // (N_HEADS // N_KV_HEADS)
                rr = pltpu.roll(rot, HEAD_DIM, axis=1)
                keep = lo if kvh == 0 else jnp.logical_not(lo)
                even = rot if kvh == 0 else rr
                odd = rr if kvh == 0 else rot
                q_ref[0, rows, LANES * (2 * j):LANES * (2 * j + 1)] = (
                    jnp.where(keep, even, 0.0).astype(BF16))
                q_ref[0, rows, LANES * (2 * j + 1):LANES * (2 * j + 2)] = (
                    jnp.where(keep, odd, 0.0).astype(BF16))
            else:
                k_ref[0, rows] = rot.astype(BF16)
        vt_ref[0, :, rows] = p[:, 640:768].T.astype(BF16)
        u_ref[0, rows] = p[:, 768:1024]
        sv_ref[0, rows] = p[:, 1024:1280].astype(BF16)
        xp_ref[0, rows] = p[:, 1280:1536]

    nsub = t // sub
    p_next = project(0)
    for r in range(nsub):
        p = p_next
        if r + 1 < nsub:
            p_next = project(r + 1)
        finish(r, p)


def _proj_call(x, sh, sc, g, w_in, e_q, e_k, gain, cos_t, sin_t, tile):
    b, l, d = x.shape
    nt = l // tile
    n_in = w_in.shape[1]
    row = lambda t_, b_: (b_, t_, 0)
    vec = lambda t_, b_: (b_, 0, 0)
    tab = lambda t_, b_: (t_, 0)
    return pl.pallas_call(
        _proj_kernel,
        grid=(nt, b),
        in_specs=[
            pl.BlockSpec((1, tile, d), row),
            pl.BlockSpec((1, 1, d), vec),
            pl.BlockSpec((1, 1, d), vec),
            _const_spec((1, d)),
            _const_spec((d, n_in)),
            _const_spec(e_q.shape),
            _const_spec(e_k.shape),
            _const_spec(gain.shape),
            pl.BlockSpec((tile, LANES), tab),
            pl.BlockSpec((tile, LANES), tab),
        ],
        out_specs=[
            pl.BlockSpec((1, tile, N_HEADS * LANES), row),
            pl.BlockSpec((1, tile, LANES), row),
            pl.BlockSpec((1, LANES, tile), lambda t_, b_: (b_, 0, t_)),
            pl.BlockSpec((1, tile, 256), row),
            pl.BlockSpec((1, tile, 256), row),
            pl.BlockSpec((1, tile, 256), row),
        ],
        out_shape=[
            jax.ShapeDtypeStruct((b, l, N_HEADS * LANES), BF16),
            jax.ShapeDtypeStruct((b, l, LANES), BF16),
            jax.ShapeDtypeStruct((b, LANES, l), BF16),
            jax.ShapeDtypeStruct((b, l, 256), F32),
            jax.ShapeDtypeStruct((b, l, 256), BF16),
            jax.ShapeDtypeStruct((b, l, 256), F32),
        ],
        compiler_params=_cparams(2),
        name="proj",
    )(x, sh, sc, g, w_in, e_q, e_k, gain, cos_t, sin_t)


def _attn_kernel(*refs, n_kv):
    q_ref = refs[0]
    k_refs = refs[1:1 + n_kv]
    vt_refs = refs[1 + n_kv:1 + 2 * n_kv]
    o_ref, s_ref, p_ref = refs[1 + 2 * n_kv:]
    group = N_HEADS // N_KV_HEADS
    nslot = s_ref.shape[0]
    sub = s_ref.shape[2]
    nunit = (q_ref.shape[1] // sub) * N_HEADS

    chunks = []
    base = 0
    for j, kr in enumerate(k_refs):
        n = kr.shape[1]
        step = min(n, ATTN_KEY_CHUNK)
        chunks += [(j, r0, step, base + r0) for r0 in range(0, n, step)]
        base += n

    def q_unit(u):
        r, i = divmod(u, N_HEADS)
        return q_ref[0, sub * r:sub * (r + 1), LANES * i:LANES * (i + 1)]

    pending = []

    def emit(u, ot, l):
        r, i = divmod(u, N_HEADS)
        kvh = i // group
        pending.append(ot[HEAD_DIM * kvh:HEAD_DIM * (kvh + 1)] * (1.0 / l))
        if i % 2 == 1:
            pair = jnp.concatenate(pending, axis=0)
            pending.clear()
            o_ref[0, sub * r:sub * (r + 1), LANES * (i // 2):LANES * (i // 2 + 1)] = (
                pair.T.astype(BF16))

    m = None
    l_prev = None
    for u in range(-1, nunit + 1):
        do_scores = u + 1 < nunit
        do_softmax = 0 <= u < nunit
        do_pv = 1 <= u
        q_next = q_unit(u + 1) if do_scores else None
        mparts, lparts, ot = [], [], None
        for j, r0, n, s0 in chunks:
            if do_scores:
                st = lax.dot_general(k_refs[j][0, r0:r0 + n], q_next, (((1,), (1,)), ((), ())),
                                     preferred_element_type=F32)
                s_ref[(u + 1) % nslot, s0:s0 + n] = st
                mparts.append(jnp.max(st, axis=0, keepdims=True))
            if do_softmax:
                p = jnp.exp2(s_ref[u % nslot, s0:s0 + n] - m)
                lparts.append(jnp.sum(p, axis=0, keepdims=True))
                p_ref[u % nslot, s0:s0 + n] = p.astype(BF16)
            if do_pv:
                d = jnp.dot(vt_refs[j][0, :, r0:r0 + n], p_ref[(u - 1) % nslot, s0:s0 + n],
                            preferred_element_type=F32)
                ot = d if ot is None else ot + d
        if do_pv:
            emit(u - 1, ot, l_prev)
        if do_softmax:
            l_prev = functools.reduce(jnp.add, lparts)
        if do_scores:
            m = functools.reduce(jnp.maximum, mparts)


def _attn_call(q, ks, vts, tq):
    b, s, _ = q.shape
    n_kv = len(ks)
    nk = sum(k.shape[1] for k in ks)
    sub = min(tq, ATTN_SUB)
    kv_map = lambda b_, t_: (b_, 0, 0)
    return pl.pallas_call(
        functools.partial(_attn_kernel, n_kv=n_kv),
        grid=(b, s // tq),
        in_specs=([pl.BlockSpec((1, tq, N_HEADS * LANES), lambda b_, t_: (b_, t_, 0))]
                  + [pl.BlockSpec((1, k.shape[1], LANES), kv_map) for k in ks]
                  + [pl.BlockSpec((1, LANES, vt.shape[2]), kv_map) for vt in vts]),
        out_specs=pl.BlockSpec((1, tq, N_HEADS * HEAD_DIM), lambda b_, t_: (b_, t_, 0)),
        out_shape=jax.ShapeDtypeStruct((b, s, N_HEADS * HEAD_DIM), BF16),
        scratch_shapes=[pltpu.VMEM((ATTN_SLOTS, nk, sub), F32),
                        pltpu.VMEM((ATTN_SLOTS, nk, sub), BF16)],
        compiler_params=_cparams(2),
        name="attn",
    )(q, *ks, *vts)


def _mix_kernel(att_ref, u_ref, sv_ref, xp_ref, xpp_ref, xpn_ref, x_ref, g1_ref,
                ws_ref, bias_ref, wp_ref, ps_ref, icnt_ref, wo_ref, o_ref):
    t = x_ref.shape[1]
    ti = pl.program_id(1)
    nt = pl.num_programs(1)
    lane = lax.broadcasted_iota(jnp.int32, (CHUNK, LANES), 1)
    lo = lane < 64

    sv = sv_ref[0]
    rows = []
    for n in range(t // CHUNK):
        cols = []
        for jp in range(N_SGU_GROUPS // 2):
            blk = sv[CHUNK * n:CHUNK * (n + 1), LANES * jp:LANES * (jp + 1)]
            r0 = jnp.dot(ws_ref[2 * jp], blk, preferred_element_type=F32)
            r1 = jnp.dot(ws_ref[2 * jp + 1], blk, preferred_element_type=F32)
            cols.append(jnp.where(lo, r0, r1))
        rows.append(jnp.concatenate(cols, axis=1) + bias_ref[...])
    sgu = u_ref[0] * jnp.concatenate(rows, axis=0)

    xp = xp_ref[0]
    prev = jnp.where(ti > 0, xpp_ref[0], 0.0)
    nxt = jnp.where(ti < nt - 1, xpn_ref[0], 0.0)
    xe = jnp.concatenate([prev, xp, nxt], axis=0)
    lane_t = lax.broadcasted_iota(jnp.int32, (t, LANES), 1)
    lo_t = lane_t < 64

    def shifted(col, k_):
        return xe[HALO + k_:HALO + k_ + t, LANES * col:LANES * (col + 1)]

    def window(col, w, inner=None, inner_w=0):
        left = w // 2
        right = w - 1 - left
        ileft = inner_w // 2
        iright = inner_w - 1 - ileft
        acc = inner
        for k_ in range(-left, right + 1):
            if inner is not None and -ileft <= k_ <= iright:
                continue
            term = shifted(col, k_)
            acc = term if acc is None else acc + term
        return acc

    w2 = window(0, 2)
    w4 = window(0, 4, w2, 2)
    w8 = window(1, 8)
    w16 = window(1, 16, w8, 8)
    tot = jnp.concatenate([jnp.where(lo_t, w2, w4), jnp.where(lo_t, w8, w16)], axis=1)
    y = tot * icnt_ref[...] - xp
    pool = jnp.dot(y.astype(BF16), wp_ref[...], preferred_element_type=F32) * ps_ref[...]

    na = att_ref.shape[2]
    ns = sgu.shape[1]
    mix = jnp.dot(att_ref[0], wo_ref[0:na, :], preferred_element_type=F32)
    mix = mix + jnp.dot(sgu.astype(BF16), wo_ref[na:na + ns, :], preferred_element_type=F32)
    mix = mix + jnp.dot(pool.astype(BF16), wo_ref[na + ns:, :], preferred_element_type=F32)
    o_ref[0] = x_ref[0] + g1_ref[0] * mix


def _mix_call(att, u, sv, xp, x, g1, ws, bias, wp_bd, pscale, icnt, w_out, tile):
    b, l, d = x.shape
    nt = l // tile
    hb = tile // HALO
    nhb = l // HALO
    row = lambda b_, t_: (b_, t_, 0)
    vec = lambda b_, t_: (b_, 0, 0)
    prev = lambda b_, t_: (b_, jnp.maximum(t_ * hb - 1, 0), 0)
    nxt = lambda b_, t_: (b_, jnp.minimum((t_ + 1) * hb, nhb - 1), 0)
    return pl.pallas_call(
        _mix_kernel,
        grid=(b, nt),
        in_specs=[
            pl.BlockSpec((1, tile, att.shape[2]), row),
            pl.BlockSpec((1, tile, 256), row),
            pl.BlockSpec((1, tile, 256), row),
            pl.BlockSpec((1, tile, 256), row),
            pl.BlockSpec((1, HALO, 256), prev),
            pl.BlockSpec((1, HALO, 256), nxt),
            pl.BlockSpec((1, tile, d), row),
            pl.BlockSpec((1, 1, d), vec),
            _const_spec(ws.shape),
            _const_spec(bias.shape),
            _const_spec(wp_bd.shape),
            _const_spec(pscale.shape),
            pl.BlockSpec((tile, 256), lambda b_, t_: (t_, 0)),
            _const_spec(w_out.shape),
        ],
        out_specs=pl.BlockSpec((1, tile, d), row),
        out_shape=jax.ShapeDtypeStruct((b, l, d), F32),
        compiler_params=_cparams(2),
        name="mix",
    )(att, u, sv, xp, xp, xp, x, g1, ws, bias, wp_bd, pscale, icnt, w_out)


def _ffn_kernel(x_ref, xp_ref, xn_ref, sh_ref, sc_ref, g2_ref, ng_ref, wg_ref, wv_ref,
                cwg_ref, cwv_ref, cbg_ref, cbv_ref, wd_ref, fg_ref, o_ref, z_ref, *, final):
    t = x_ref.shape[1]
    ti = pl.program_id(1)
    nt = pl.num_programs(1)
    x = x_ref[0]
    ng = ng_ref[...]
    sh = sh_ref[0]
    sc = sc_ref[0]
    hprev = jnp.where(ti > 0, _rms_mod(xp_ref[0], ng, sh, sc), 0.0)
    hnext = jnp.where(ti < nt - 1, _rms_mod(xn_ref[0], ng, sh, sc), 0.0)
    h = jnp.concatenate([hprev, _rms_mod(x, ng, sh, sc), hnext], axis=0).astype(BF16)
    nch = wg_ref.shape[0]

    nl = wg_ref.shape[2] // LANES
    nslot = z_ref.shape[0]

    def up(c):
        slot = c % nslot
        zg = jnp.dot(h, wg_ref[c], preferred_element_type=F32)
        zv = jnp.dot(h, wv_ref[c], preferred_element_type=F32)
        for j in range(nl):
            z_ref[slot, j] = zg[:, LANES * j:LANES * (j + 1)]
            z_ref[slot, nl + j] = zv[:, LANES * j:LANES * (j + 1)]

    def conv(slot, j, cw, cb):
        return (z_ref[slot, j, HALO - 1:HALO - 1 + t] * cw[0:1]
                + z_ref[slot, j, HALO:HALO + t] * cw[1:2]
                + z_ref[slot, j, HALO + 1:HALO + 1 + t] * cw[2:3] + cb)

    acc = jnp.zeros((t, x.shape[1]), F32)
    for c in range(min(nslot - 1, nch)):
        up(c)
    for c in range(nch):
        if c + nslot - 1 < nch:
            up(c + nslot - 1)
        slot = c % nslot
        parts = []
        for j in range(nl):
            ls = slice(LANES * j, LANES * (j + 1))
            cg = conv(slot, j, cwg_ref[c][:, ls], cbg_ref[c][:, ls])
            cv = conv(slot, nl + j, cwv_ref[c][:, ls], cbv_ref[c][:, ls])
            parts.append((_silu(cg) * cv).astype(BF16))
        a = jnp.concatenate(parts, axis=1)
        acc = acc + jnp.dot(a, wd_ref[c], preferred_element_type=F32)
    y = x + g2_ref[0] * acc
    if final:
        ms = jnp.mean(y * y, axis=-1, keepdims=True)
        y = y * lax.rsqrt(ms + EPS) * fg_ref[...]
    o_ref[0] = y


def _ffn_call(x, sh, sc, g2, ng, wg, wv, cwg, cwv, cbg, cbv, wd, fg, tile, final):
    b, l, d = x.shape
    nt = l // tile
    hb = tile // HALO
    nhb = l // HALO
    row = lambda b_, t_: (b_, t_, 0)
    vec = lambda b_, t_: (b_, 0, 0)
    prev = lambda b_, t_: (b_, jnp.maximum(t_ * hb - 1, 0), 0)
    nxt = lambda b_, t_: (b_, jnp.minimum((t_ + 1) * hb, nhb - 1), 0)
    return pl.pallas_call(
        functools.partial(_ffn_kernel, final=final),
        grid=(b, nt),
        in_specs=[
            pl.BlockSpec((1, tile, d), row),
            pl.BlockSpec((1, HALO, d), prev),
            pl.BlockSpec((1, HALO, d), nxt),
            pl.BlockSpec((1, 1, d), vec),
            pl.BlockSpec((1, 1, d), vec),
            pl.BlockSpec((1, 1, d), vec),
            _const_spec(ng.shape),
            _const_spec(wg.shape),
            _const_spec(wv.shape),
            _const_spec(cwg.shape),
            _const_spec(cwv.shape),
            _const_spec(cbg.shape),
            _const_spec(cbv.shape),
            _const_spec(wd.shape),
            _const_spec(fg.shape),
        ],
        out_specs=pl.BlockSpec((1, tile, d), row),
        out_shape=jax.ShapeDtypeStruct((b, l, d), F32),
        scratch_shapes=[pltpu.VMEM((FFN_Z_SLOTS, 2 * wg.shape[2] // LANES, tile + 2 * HALO, LANES), F32)],
        compiler_params=_cparams(2),
        name="ffn_final" if final else "ffn",
    )(x, x, x, sh, sc, g2, ng, wg, wv, cwg, cwv, cbg, cbv, wd, fg)


def _rope_tables(n):
    pos = jnp.arange(n)
    row = (pos // GRID_W).astype(F32)
    col = (pos % GRID_W).astype(F32)
    axis_dim = HEAD_DIM // 2
    inv = ROPE_BASE ** (-jnp.arange(0, axis_dim, 2, dtype=F32) / axis_dim)
    ang_r = row[:, None] * inv[None, :]
    ang_c = col[:, None] * inv[None, :]
    cr, sr, cc, sn = jnp.cos(ang_r), jnp.sin(ang_r), jnp.cos(ang_c), jnp.sin(ang_c)
    cos_h = jnp.concatenate([cr, cr, cc, cc], axis=1)
    sin_h = jnp.concatenate([-sr, sr, -sn, sn], axis=1)
    return jnp.tile(cos_h, (1, 2)), jnp.tile(sin_h, (1, 2))


def _pool_inv_count(l):
    t = np.arange(l)
    cols = []
    for w in POOL_WINDOWS:
        left = w // 2
        right = w - 1 - left
        cnt = (np.minimum(t + right, l - 1) - np.maximum(t - left, 0) + 1).astype(np.float32)
        cols.append(np.repeat((np.float32(1.0) / cnt)[:, None], POOL_GROUP_DIM, axis=1))
    return jnp.asarray(np.concatenate(cols, axis=1))


def _block_diag(w):
    g, a, b = w.shape
    out = jnp.zeros((g * a, g * b), w.dtype)
    for i in range(g):
        out = out.at[i * a:(i + 1) * a, i * b:(i + 1) * b].set(w[i])
    return out


def _head_mean_matrix(n_heads):
    n = n_heads * HEAD_DIM
    idx = np.arange(n) // HEAD_DIM
    return jnp.asarray((idx[:, None] == idx[None, :]).astype(np.float32) / HEAD_DIM, dtype=BF16)


def _split6(m):
    return [m[..., i * (m.shape[-1] // 6):(i + 1) * (m.shape[-1] // 6)] for i in range(6)]


def kernel(x, c, ctx, c_ctx, w_mod, b_mod, norm1_g, w_in, q_gain, k_gain, w_s, b_s, w_pool,
           pool_scale, w_out, norm2_g, w_up, conv_w, conv_b, w_down, final_g):
    b, s, d = x.shape
    cl = ctx.shape[1]
    depth = w_mod.shape[0]
    d_ff = w_down.shape[1]
    fc = 256
    nch = d_ff // fc

    rows = ((b + 1 + SUBLANES - 1) // SUBLANES) * SUBLANES
    cvec = jnp.zeros((rows, d), F32).at[:b].set(c).at[b].set(c_ctx)
    mod = _mod_call(cvec, w_mod, b_mod)

    cos_x, sin_x = _rope_tables(s)
    cos_c = jnp.ones((cl, LANES), F32)
    sin_c = jnp.zeros((cl, LANES), F32)
    e_q = _head_mean_matrix(N_HEADS)
    e_k = _head_mean_matrix(N_KV_HEADS)
    icnt_x = _pool_inv_count(s)
    icnt_c = _pool_inv_count(cl)
    fg = final_g.reshape(1, d)

    tile_c = min(256, cl)
    big_x = min(512, s)

    for i in range(depth):
        last = i == depth - 1
        mx = [m.reshape(b, 1, d) for m in _split6(mod[i, :b])]
        mc = [jnp.broadcast_to(m.reshape(1, 1, d), (b, 1, d)) for m in _split6(mod[i, b])]
        g1n = norm1_g[i].reshape(1, d)
        g2n = norm2_g[i].reshape(1, d)
        w_in_b = w_in[i].astype(BF16)
        w_out_b = w_out[i].astype(BF16)
        gain = jnp.concatenate([jnp.tile(q_gain[i] * (HEAD_DIM ** -0.5 * math.log2(math.e)), N_HEADS),
                                jnp.tile(k_gain[i], N_KV_HEADS)]).reshape(1, -1)
        ws_b = w_s[i].astype(BF16)
        bias = jnp.repeat(b_s[i].T, d // 4 // N_SGU_GROUPS, axis=1)
        wp_bd = _block_diag(w_pool[i]).astype(BF16)
        pscale = pool_scale[i].reshape(1, -1)
        wup = w_up[i].astype(BF16)
        wg = wup[:, :d_ff].reshape(d, nch, fc).transpose(1, 0, 2)
        wv = wup[:, d_ff:].reshape(d, nch, fc).transpose(1, 0, 2)
        cwg = conv_w[i][:, :d_ff].reshape(3, nch, fc).transpose(1, 0, 2)
        cwv = conv_w[i][:, d_ff:].reshape(3, nch, fc).transpose(1, 0, 2)
        cbg = conv_b[i][:d_ff].reshape(nch, 1, fc)
        cbv = conv_b[i][d_ff:].reshape(nch, 1, fc)
        wd = w_down[i].astype(BF16).reshape(nch, fc, d)

        qx, kx, vtx, ux, svx, xpx = _proj_call(x, mx[0], mx[1], g1n, w_in_b, e_q, e_k, gain,
                                               cos_x, sin_x, big_x)
        qc, kc, vtc, uc, svc, xpc = _proj_call(ctx, mc[0], mc[1], g1n, w_in_b, e_q, e_k, gain,
                                               cos_c, sin_c, tile_c)
        att_x = _attn_call(qx, [kx, kc], [vtx, vtc], big_x)
        x = _mix_call(att_x, ux, svx, xpx, x, mx[2], ws_b, bias, wp_bd, pscale, icnt_x,
                      w_out_b, big_x)
        x = _ffn_call(x, mx[3], mx[4], mx[5], g2n, wg, wv, cwg, cwv, cbg, cbv, wd, fg,
                      big_x, last)
        if not last:
            att_c = _attn_call(qc, [kc], [vtc], tile_c)
            ctx = _mix_call(att_c, uc, svc, xpc, ctx, mc[2], ws_b, bias, wp_bd, pscale, icnt_c,
                            w_out_b, tile_c)
            ctx = _ffn_call(ctx, mc[3], mc[4], mc[5], g2n, wg, wv, cwg, cwv, cbg, cbv, wd, fg,
                            tile_c, False)
    return x
```

```python
import functools
import math

import numpy as np
import jax
import jax.numpy as jnp
from jax import lax
from jax.experimental import pallas as pl
from jax.experimental.pallas import tpu as pltpu

F32 = jnp.float32
BF16 = jnp.bfloat16

EPS = 1e-6
GRID_W = 64
HEAD_DIM = 64
N_HEADS = 8
N_KV_HEADS = 2
ROPE_BASE = 10000.0
CHUNK = 128
N_SGU_GROUPS = 4
POOL_WINDOWS = (2, 4, 8, 16)
POOL_GROUP_DIM = 64
LANES = 128
SUBLANES = 8
HALO = SUBLANES
FFN_HALO = 2 * SUBLANES
VMEM_LIMIT = 56 * 1024 * 1024
FFN_Z_SLOTS = 3
ATTN_SLOTS = 2
ATTN_SUB = 256
ATTN_KEY_CHUNK = 512
PROJ_SUB = 256


def _cparams(n_axes):
    return pltpu.CompilerParams(
        dimension_semantics=("arbitrary",) * n_axes, vmem_limit_bytes=VMEM_LIMIT)


def _const_spec(shape):
    nd = len(shape)
    return pl.BlockSpec(shape, lambda *_: (0,) * nd, pipeline_mode=pl.Buffered(1))


def _silu(x):
    return x * (1.0 / (1.0 + jnp.exp(-x)))


def _rms_mod(x, g, shift, scale):
    ms = jnp.mean(x * x, axis=-1, keepdims=True)
    y = x * lax.rsqrt(ms + EPS) * g
    return y * (1.0 + scale) + shift


def _mod_kernel(c_ref, w_ref, b_ref, o_ref):
    s = _silu(c_ref[...]).astype(BF16)
    o_ref[0] = jnp.dot(s, w_ref[0].astype(BF16), preferred_element_type=F32) + b_ref[0]


def _mod_call(cvec, w_mod, b_mod):
    depth, d, n = w_mod.shape
    rows = cvec.shape[0]
    tn = 1536
    return pl.pallas_call(
        _mod_kernel,
        grid=(depth, n // tn),
        in_specs=[
            pl.BlockSpec((rows, d), lambda i, j: (0, 0)),
            pl.BlockSpec((1, d, tn), lambda i, j: (i, 0, j)),
            pl.BlockSpec((1, 1, tn), lambda i, j: (i, 0, j)),
        ],
        out_specs=pl.BlockSpec((1, rows, tn), lambda i, j: (i, 0, j)),
        out_shape=jax.ShapeDtypeStruct((depth, rows, n), F32),
        compiler_params=_cparams(2),
        name="mod",
    )(cvec, w_mod, b_mod.reshape(depth, 1, n))


def _proj_kernel(x_ref, sh_ref, sc_ref, g_ref, w_ref, eq_ref, ek_ref, gain_ref, cos_ref, sin_ref,
                 q_ref, k_ref, vt_ref, u_ref, sv_ref, xp_ref):
    t = x_ref.shape[1]
    sub = min(t, PROJ_SUB)
    nq = N_HEADS * HEAD_DIM
    nqk = (N_HEADS + N_KV_HEADS) * HEAD_DIM
    lane = lax.broadcasted_iota(jnp.int32, (sub, LANES), 1)
    first = (lane % 32) < 16
    lo = lane < HEAD_DIM

    def project(r):
        rows = slice(sub * r, sub * (r + 1))
        h = _rms_mod(x_ref[0, rows], g_ref[...], sh_ref[0], sc_ref[0]).astype(BF16)
        return jnp.dot(h, w_ref[...], preferred_element_type=F32)

    def finish(r, p):
        rows = slice(sub * r, sub * (r + 1))
        qk = p[:, :nqk]
        sq = (qk * qk).astype(BF16)
        msh = jnp.concatenate(
            [jnp.dot(sq[:, :nq], eq_ref[...], preferred_element_type=F32),
             jnp.dot(sq[:, nq:], ek_ref[...], preferred_element_type=F32)], axis=1)
        qkn = qk * lax.rsqrt(msh + EPS) * gain_ref[...]
        c = cos_ref[rows]
        s = sin_ref[rows]
        for j in range(nqk // LANES):
            blk = qkn[:, LANES * j:LANES * (j + 1)]
            sw = jnp.where(first, pltpu.roll(blk, LANES - 16, axis=1), pltpu.roll(blk, 16, axis=1))
            rot = blk * c + sw * s
            if j < N_HEADS // 2:
                kvh = (2 * j) // (N_HEADS // N_KV_HEADS)
                rr = pltpu.roll(rot, HEAD_DIM, axis=1)
                keep = lo if kvh == 0 else jnp.logical_not(lo)
                even = rot if kvh == 0 else rr
                odd = rr if kvh == 0 else rot
                q_ref[0, rows, LANES * (2 * j):LANES * (2 * j + 1)] = (
                    jnp.where(keep, even, 0.0).astype(BF16))
                q_ref[0, rows, LANES * (2 * j + 1):LANES * (2 * j + 2)] = (
                    jnp.where(keep, odd, 0.0).astype(BF16))
            else:
                k_ref[0, rows] = rot.astype(BF16)
        vt_ref[0, :, rows] = p[:, 640:768].T.astype(BF16)
        u_ref[0, rows] = p[:, 768:1024]
        sv_ref[0, rows] = p[:, 1024:1280].astype(BF16)
        xp_ref[0, rows] = p[:, 1280:1536]

    nsub = t // sub
    p_next = project(0)
    for r in range(nsub):
        p = p_next
        if r + 1 < nsub:
            p_next = project(r + 1)
        finish(r, p)


def _proj_call(x, sh, sc, g, w_in, e_q, e_k, gain, cos_t, sin_t, tile):
    b, l, d = x.shape
    nt = l // tile
    n_in = w_in.shape[1]
    row = lambda t_, b_: (b_, t_, 0)
    vec = lambda t_, b_: (b_, 0, 0)
    tab = lambda t_, b_: (t_, 0)
    return pl.pallas_call(
        _proj_kernel,
        grid=(nt, b),
        in_specs=[
            pl.BlockSpec((1, tile, d), row),
            pl.BlockSpec((1, 1, d), vec),
            pl.BlockSpec((1, 1, d), vec),
            _const_spec((1, d)),
            _const_spec((d, n_in)),
            _const_spec(e_q.shape),
            _const_spec(e_k.shape),
            _const_spec(gain.shape),
            pl.BlockSpec((tile, LANES), tab),
            pl.BlockSpec((tile, LANES), tab),
        ],
        out_specs=[
            pl.BlockSpec((1, tile, N_HEADS * LANES), row),
            pl.BlockSpec((1, tile, LANES), row),
            pl.BlockSpec((1, LANES, tile), lambda t_, b_: (b_, 0, t_)),
            pl.BlockSpec((1, tile, 256), row),
            pl.BlockSpec((1, tile, 256), row),
            pl.BlockSpec((1, tile, 256), row),
        ],
        out_shape=[
            jax.ShapeDtypeStruct((b, l, N_HEADS * LANES), BF16),
            jax.ShapeDtypeStruct((b, l, LANES), BF16),
            jax.ShapeDtypeStruct((b, LANES, l), BF16),
            jax.ShapeDtypeStruct((b, l, 256), F32),
            jax.ShapeDtypeStruct((b, l, 256), BF16),
            jax.ShapeDtypeStruct((b, l, 256), F32),
        ],
        compiler_params=_cparams(2),
        name="proj",
    )(x, sh, sc, g, w_in, e_q, e_k, gain, cos_t, sin_t)


def _attn_kernel(*refs, n_kv):
    q_ref = refs[0]
    k_refs = refs[1:1 + n_kv]
    vt_refs = refs[1 + n_kv:1 + 2 * n_kv]
    o_ref, s_ref, p_ref = refs[1 + 2 * n_kv:]
    group = N_HEADS // N_KV_HEADS
    nslot = s_ref.shape[0]
    sub = s_ref.shape[2]
    nunit = (q_ref.shape[1] // sub) * N_HEADS

    chunks = []
    base = 0
    for j, kr in enumerate(k_refs):
        n = kr.shape[1]
        step = min(n, ATTN_KEY_CHUNK)
        chunks += [(j, r0, step, base + r0) for r0 in range(0, n, step)]
        base += n

    def q_unit(u):
        r, i = divmod(u, N_HEADS)
        return q_ref[0, sub * r:sub * (r + 1), LANES * i:LANES * (i + 1)]

    def v_ext(kvh, j, r0, n):
        vt = vt_refs[j][0, HEAD_DIM * kvh:HEAD_DIM * (kvh + 1), r0:r0 + n]
        return jnp.concatenate([vt, jnp.ones((2 * SUBLANES, n), BF16)], axis=0)

    pending = []

    def emit(u, ot):
        r, i = divmod(u, N_HEADS)
        l = ot[HEAD_DIM:HEAD_DIM + 1]
        pending.append(ot[:HEAD_DIM] * (1.0 / l))
        if i % 2 == 1:
            pair = jnp.concatenate(pending, axis=0)
            pending.clear()
            o_ref[0, sub * r:sub * (r + 1), LANES * (i // 2):LANES * (i // 2 + 1)] = (
                pair.T.astype(BF16))

    m = None
    for u in range(-1, nunit + 1):
        do_scores = u + 1 < nunit
        do_softmax = 0 <= u < nunit
        do_pv = 1 <= u
        q_next = q_unit(u + 1) if do_scores else None
        kvh_prev = ((u - 1) % N_HEADS) // group
        mparts, ot = [], None
        for j, r0, n, s0 in chunks:
            if do_scores:
                st = lax.dot_general(k_refs[j][0, r0:r0 + n], q_next, (((1,), (1,)), ((), ())),
                                     preferred_element_type=F32)
                s_ref[(u + 1) % nslot, s0:s0 + n] = st
                mparts.append(jnp.max(st, axis=0, keepdims=True))
            if do_softmax:
                p = jnp.exp2(s_ref[u % nslot, s0:s0 + n] - m)
                p_ref[u % nslot, s0:s0 + n] = p.astype(BF16)
            if do_pv:
                d = jnp.dot(v_ext(kvh_prev, j, r0, n), p_ref[(u - 1) % nslot, s0:s0 + n],
                            preferred_element_type=F32)
                ot = d if ot is None else ot + d
        if do_pv:
            emit(u - 1, ot)
        if do_scores:
            m = functools.reduce(jnp.maximum, mparts)


def _attn_call(q, ks, vts, tq):
    b, s, _ = q.shape
    n_kv = len(ks)
    nk = sum(k.shape[1] for k in ks)
    sub = min(tq, ATTN_SUB)
    kv_map = lambda b_, t_: (b_, 0, 0)
    return pl.pallas_call(
        functools.partial(_attn_kernel, n_kv=n_kv),
        grid=(b, s // tq),
        in_specs=([pl.BlockSpec((1, tq, N_HEADS * LANES), lambda b_, t_: (b_, t_, 0))]
                  + [pl.BlockSpec((1, k.shape[1], LANES), kv_map) for k in ks]
                  + [pl.BlockSpec((1, LANES, vt.shape[2]), kv_map) for vt in vts]),
        out_specs=pl.BlockSpec((1, tq, N_HEADS * HEAD_DIM), lambda b_, t_: (b_, t_, 0)),
        out_shape=jax.ShapeDtypeStruct((b, s, N_HEADS * HEAD_DIM), BF16),
        scratch_shapes=[pltpu.VMEM((ATTN_SLOTS, nk, sub), F32),
                        pltpu.VMEM((ATTN_SLOTS, nk, sub), BF16)],
        compiler_params=_cparams(2),
        name="attn",
    )(q, *ks, *vts)


def _mix_kernel(att_ref, u_ref, sv_ref, xp_ref, xpp_ref, xpn_ref, x_ref, g1_ref,
                ws_ref, bias_ref, wp_ref, ps_ref, icnt_ref, wo_ref, sh2_ref, sc2_ref, ng2_ref,
                o_ref, h2_ref):
    t = x_ref.shape[1]
    ti = pl.program_id(1)
    nt = pl.num_programs(1)
    lane = lax.broadcasted_iota(jnp.int32, (CHUNK, LANES), 1)
    lo = lane < 64

    sv = sv_ref[0]
    rows = []
    for n in range(t // CHUNK):
        cols = []
        for jp in range(N_SGU_GROUPS // 2):
            blk = sv[CHUNK * n:CHUNK * (n + 1), LANES * jp:LANES * (jp + 1)]
            r0 = jnp.dot(ws_ref[2 * jp], blk, preferred_element_type=F32)
            r1 = jnp.dot(ws_ref[2 * jp + 1], blk, preferred_element_type=F32)
            cols.append(jnp.where(lo, r0, r1))
        rows.append(jnp.concatenate(cols, axis=1) + bias_ref[...])
    sgu = u_ref[0] * jnp.concatenate(rows, axis=0)

    xp = xp_ref[0]
    prev = jnp.where(ti > 0, xpp_ref[0], 0.0)
    nxt = jnp.where(ti < nt - 1, xpn_ref[0], 0.0)
    xe = jnp.concatenate([prev, xp, nxt], axis=0)
    lane_t = lax.broadcasted_iota(jnp.int32, (t, LANES), 1)
    lo_t = lane_t < 64

    def shifted(col, k_):
        return xe[HALO + k_:HALO + k_ + t, LANES * col:LANES * (col + 1)]

    def window(col, w, inner=None, inner_w=0):
        left = w // 2
        right = w - 1 - left
        ileft = inner_w // 2
        iright = inner_w - 1 - ileft
        acc = inner
        for k_ in range(-left, right + 1):
            if inner is not None and -ileft <= k_ <= iright:
                continue
            term = shifted(col, k_)
            acc = term if acc is None else acc + term
        return acc

    w2 = window(0, 2)
    w4 = window(0, 4, w2, 2)
    w8 = window(1, 8)
    w16 = window(1, 16, w8, 8)
    tot = jnp.concatenate([jnp.where(lo_t, w2, w4), jnp.where(lo_t, w8, w16)], axis=1)
    y = tot * icnt_ref[...] - xp
    pool = jnp.dot(y.astype(BF16), wp_ref[...], preferred_element_type=F32) * ps_ref[...]

    na = att_ref.shape[2]
    ns = sgu.shape[1]
    mix = jnp.dot(att_ref[0], wo_ref[0:na, :], preferred_element_type=F32)
    mix = mix + jnp.dot(sgu.astype(BF16), wo_ref[na:na + ns, :], preferred_element_type=F32)
    mix = mix + jnp.dot(pool.astype(BF16), wo_ref[na + ns:, :], preferred_element_type=F32)
    xn = x_ref[0] + g1_ref[0] * mix
    o_ref[0] = xn
    h2_ref[0] = _rms_mod(xn, ng2_ref[...], sh2_ref[0], sc2_ref[0]).astype(BF16)


def _mix_call(att, u, sv, xp, x, g1, ws, bias, wp_bd, pscale, icnt, w_out, sh2, sc2, ng2, tile):
    b, l, d = x.shape
    nt = l // tile
    hb = tile // HALO
    nhb = l // HALO
    row = lambda b_, t_: (b_, t_, 0)
    vec = lambda b_, t_: (b_, 0, 0)
    prev = lambda b_, t_: (b_, jnp.maximum(t_ * hb - 1, 0), 0)
    nxt = lambda b_, t_: (b_, jnp.minimum((t_ + 1) * hb, nhb - 1), 0)
    return pl.pallas_call(
        _mix_kernel,
        grid=(b, nt),
        in_specs=[
            pl.BlockSpec((1, tile, att.shape[2]), row),
            pl.BlockSpec((1, tile, 256), row),
            pl.BlockSpec((1, tile, 256), row),
            pl.BlockSpec((1, tile, 256), row),
            pl.BlockSpec((1, HALO, 256), prev),
            pl.BlockSpec((1, HALO, 256), nxt),
            pl.BlockSpec((1, tile, d), row),
            pl.BlockSpec((1, 1, d), vec),
            _const_spec(ws.shape),
            _const_spec(bias.shape),
            _const_spec(wp_bd.shape),
            _const_spec(pscale.shape),
            pl.BlockSpec((tile, 256), lambda b_, t_: (t_, 0)),
            _const_spec(w_out.shape),
            pl.BlockSpec((1, 1, d), vec),
            pl.BlockSpec((1, 1, d), vec),
            _const_spec(ng2.shape),
        ],
        out_specs=[pl.BlockSpec((1, tile, d), row), pl.BlockSpec((1, tile, d), row)],
        out_shape=[jax.ShapeDtypeStruct((b, l, d), F32), jax.ShapeDtypeStruct((b, l, d), BF16)],
        compiler_params=_cparams(2),
        name="mix",
    )(att, u, sv, xp, xp, xp, x, g1, ws, bias, wp_bd, pscale, icnt, w_out, sh2, sc2, ng2)


def _ffn_kernel(x_ref, h_ref, hp_ref, hn_ref, g2_ref, wg_ref, wv_ref,
                cwg_ref, cwv_ref, cbg_ref, cbv_ref, wd_ref, fg_ref, o_ref, z_ref, *, final):
    t = x_ref.shape[1]
    ti = pl.program_id(1)
    nt = pl.num_programs(1)
    x = x_ref[0]
    hprev = jnp.where(ti > 0, hp_ref[0], jnp.zeros_like(hp_ref[0]))
    hnext = jnp.where(ti < nt - 1, hn_ref[0], jnp.zeros_like(hn_ref[0]))
    h = jnp.concatenate([hprev, h_ref[0], hnext], axis=0)
    nch = wg_ref.shape[0]

    nl = wg_ref.shape[2] // LANES
    nslot = z_ref.shape[0]

    def up(c):
        slot = c % nslot
        zg = jnp.dot(h, wg_ref[c], preferred_element_type=F32)
        zv = jnp.dot(h, wv_ref[c], preferred_element_type=F32)
        for j in range(nl):
            z_ref[slot, j] = zg[:, LANES * j:LANES * (j + 1)]
            z_ref[slot, nl + j] = zv[:, LANES * j:LANES * (j + 1)]

    def conv(slot, j, cw, cb):
        return (z_ref[slot, j, FFN_HALO - 1:FFN_HALO - 1 + t] * cw[0:1]
                + z_ref[slot, j, FFN_HALO:FFN_HALO + t] * cw[1:2]
                + z_ref[slot, j, FFN_HALO + 1:FFN_HALO + 1 + t] * cw[2:3] + cb)

    acc = jnp.zeros((t, x.shape[1]), F32)
    for c in range(min(nslot - 1, nch)):
        up(c)
    for c in range(nch):
        if c + nslot - 1 < nch:
            up(c + nslot - 1)
        slot = c % nslot
        parts = []
        for j in range(nl):
            ls = slice(LANES * j, LANES * (j + 1))
            cg = conv(slot, j, cwg_ref[c][:, ls], cbg_ref[c][:, ls])
            cv = conv(slot, nl + j, cwv_ref[c][:, ls], cbv_ref[c][:, ls])
            parts.append((_silu(cg) * cv).astype(BF16))
        a = jnp.concatenate(parts, axis=1)
        acc = acc + jnp.dot(a, wd_ref[c], preferred_element_type=F32)
    y = x + g2_ref[0] * acc
    if final:
        ms = jnp.mean(y * y, axis=-1, keepdims=True)
        y = y * lax.rsqrt(ms + EPS) * fg_ref[...]
    o_ref[0] = y


def _ffn_call(x, h2, g2, wg, wv, cwg, cwv, cbg, cbv, wd, fg, tile, final):
    b, l, d = x.shape
    nt = l // tile
    hb = tile // FFN_HALO
    nhb = l // FFN_HALO
    row = lambda b_, t_: (b_, t_, 0)
    vec = lambda b_, t_: (b_, 0, 0)
    prev = lambda b_, t_: (b_, jnp.maximum(t_ * hb - 1, 0), 0)
    nxt = lambda b_, t_: (b_, jnp.minimum((t_ + 1) * hb, nhb - 1), 0)
    return pl.pallas_call(
        functools.partial(_ffn_kernel, final=final),
        grid=(b, nt),
        in_specs=[
            pl.BlockSpec((1, tile, d), row),
            pl.BlockSpec((1, tile, d), row),
            pl.BlockSpec((1, FFN_HALO, d), prev),
            pl.BlockSpec((1, FFN_HALO, d), nxt),
            pl.BlockSpec((1, 1, d), vec),
            _const_spec(wg.shape),
            _const_spec(wv.shape),
            _const_spec(cwg.shape),
            _const_spec(cwv.shape),
            _const_spec(cbg.shape),
            _const_spec(cbv.shape),
            _const_spec(wd.shape),
            _const_spec(fg.shape),
        ],
        out_specs=pl.BlockSpec((1, tile, d), row),
        out_shape=jax.ShapeDtypeStruct((b, l, d), F32),
        scratch_shapes=[pltpu.VMEM(
            (FFN_Z_SLOTS, 2 * wg.shape[2] // LANES, tile + 2 * FFN_HALO, LANES), F32)],
        compiler_params=_cparams(2),
        name="ffn_final" if final else "ffn",
    )(x, h2, h2, h2, g2, wg, wv, cwg, cwv, cbg, cbv, wd, fg)


def _rope_tables(n):
    pos = jnp.arange(n)
    row = (pos // GRID_W).astype(F32)
    col = (pos % GRID_W).astype(F32)
    axis_dim = HEAD_DIM // 2
    inv = ROPE_BASE ** (-jnp.arange(0, axis_dim, 2, dtype=F32) / axis_dim)
    ang_r = row[:, None] * inv[None, :]
    ang_c = col[:, None] * inv[None, :]
    cr, sr, cc, sn = jnp.cos(ang_r), jnp.sin(ang_r), jnp.cos(ang_c), jnp.sin(ang_c)
    cos_h = jnp.concatenate([cr, cr, cc, cc], axis=1)
    sin_h = jnp.concatenate([-sr, sr, -sn, sn], axis=1)
    return jnp.tile(cos_h, (1, 2)), jnp.tile(sin_h, (1, 2))


def _pool_inv_count(l):
    t = np.arange(l)
    cols = []
    for w in POOL_WINDOWS:
        left = w // 2
        right = w - 1 - left
        cnt = (np.minimum(t + right, l - 1) - np.maximum(t - left, 0) + 1).astype(np.float32)
        cols.append(np.repeat((np.float32(1.0) / cnt)[:, None], POOL_GROUP_DIM, axis=1))
    return jnp.asarray(np.concatenate(cols, axis=1))


def _block_diag(w):
    g, a, b = w.shape
    out = jnp.zeros((g * a, g * b), w.dtype)
    for i in range(g):
        out = out.at[i * a:(i + 1) * a, i * b:(i + 1) * b].set(w[i])
    return out


def _head_mean_matrix(n_heads):
    n = n_heads * HEAD_DIM
    idx = np.arange(n) // HEAD_DIM
    return jnp.asarray((idx[:, None] == idx[None, :]).astype(np.float32) / HEAD_DIM, dtype=BF16)


def _split6(m):
    return [m[..., i * (m.shape[-1] // 6):(i + 1) * (m.shape[-1] // 6)] for i in range(6)]


def kernel(x, c, ctx, c_ctx, w_mod, b_mod, norm1_g, w_in, q_gain, k_gain, w_s, b_s, w_pool,
           pool_scale, w_out, norm2_g, w_up, conv_w, conv_b, w_down, final_g):
    b, s, d = x.shape
    cl = ctx.shape[1]
    depth = w_mod.shape[0]
    d_ff = w_down.shape[1]
    fc = 256
    nch = d_ff // fc

    rows = ((b + 1 + SUBLANES - 1) // SUBLANES) * SUBLANES
    cvec = jnp.zeros((rows, d), F32).at[:b].set(c).at[b].set(c_ctx)
    mod = _mod_call(cvec, w_mod, b_mod)

    cos_x, sin_x = _rope_tables(s)
    cos_c = jnp.ones((cl, LANES), F32)
    sin_c = jnp.zeros((cl, LANES), F32)
    e_q = _head_mean_matrix(N_HEADS)
    e_k = _head_mean_matrix(N_KV_HEADS)
    icnt_x = _pool_inv_count(s)
    icnt_c = _pool_inv_count(cl)
    fg = final_g.reshape(1, d)

    tile_c = min(256, cl)
    big_x = min(512, s)

    for i in range(depth):
        last = i == depth - 1
        mx = [m.reshape(b, 1, d) for m in _split6(mod[i, :b])]
        mc = [jnp.broadcast_to(m.reshape(1, 1, d), (b, 1, d)) for m in _split6(mod[i, b])]
        g1n = norm1_g[i].reshape(1, d)
        g2n = norm2_g[i].reshape(1, d)
        w_in_b = w_in[i].astype(BF16)
        w_out_b = w_out[i].astype(BF16)
        gain = jnp.concatenate([jnp.tile(q_gain[i] * (HEAD_DIM ** -0.5 * math.log2(math.e)), N_HEADS),
                                jnp.tile(k_gain[i], N_KV_HEADS)]).reshape(1, -1)
        ws_b = w_s[i].astype(BF16)
        bias = jnp.repeat(b_s[i].T, d // 4 // N_SGU_GROUPS, axis=1)
        wp_bd = _block_diag(w_pool[i]).astype(BF16)
        pscale = pool_scale[i].reshape(1, -1)
        wup = w_up[i].astype(BF16)
        wg = wup[:, :d_ff].reshape(d, nch, fc).transpose(1, 0, 2)
        wv = wup[:, d_ff:].reshape(d, nch, fc).transpose(1, 0, 2)
        cwg = conv_w[i][:, :d_ff].reshape(3, nch, fc).transpose(1, 0, 2)
        cwv = conv_w[i][:, d_ff:].reshape(3, nch, fc).transpose(1, 0, 2)
        cbg = conv_b[i][:d_ff].reshape(nch, 1, fc)
        cbv = conv_b[i][d_ff:].reshape(nch, 1, fc)
        wd = w_down[i].astype(BF16).reshape(nch, fc, d)

        qx, kx, vtx, ux, svx, xpx = _proj_call(x, mx[0], mx[1], g1n, w_in_b, e_q, e_k, gain,
                                               cos_x, sin_x, big_x)
        qc, kc, vtc, uc, svc, xpc = _proj_call(ctx, mc[0], mc[1], g1n, w_in_b, e_q, e_k, gain,
                                               cos_c, sin_c, tile_c)
        att_x = _attn_call(qx, [kx, kc], [vtx, vtc], big_x)
        x, h2x = _mix_call(att_x, ux, svx, xpx, x, mx[2], ws_b, bias, wp_bd, pscale, icnt_x,
                           w_out_b, mx[3], mx[4], g2n, big_x)
        x = _ffn_call(x, h2x, mx[5], wg, wv, cwg, cwv, cbg, cbv, wd, fg, big_x, last)
        if not last:
            att_c = _attn_call(qc, [kc], [vtc], tile_c)
            ctx, h2c = _mix_call(att_c, uc, svc, xpc, ctx, mc[2], ws_b, bias, wp_bd, pscale,
                                 icnt_c, w_out_b, mc[3], mc[4], g2n, tile_c)
            ctx = _ffn_call(ctx, h2c, mc[5], wg, wv, cwg, cwv, cbg, cbv, wd, fg, tile_c, False)
    return x
```

```python
import functools
import math

import numpy as np
import jax
import jax.numpy as jnp
from jax import lax
from jax.experimental import pallas as pl
from jax.experimental.pallas import tpu as pltpu

F32 = jnp.float32
BF16 = jnp.bfloat16

EPS = 1e-6
GRID_W = 64
HEAD_DIM = 64
N_HEADS = 8
N_KV_HEADS = 2
ROPE_BASE = 10000.0
CHUNK = 128
N_SGU_GROUPS = 4
POOL_WINDOWS = (2, 4, 8, 16)
POOL_GROUP_DIM = 64
LANES = 128
SUBLANES = 8
HALO = SUBLANES
VMEM_LIMIT = 56 * 1024 * 1024
FFN_Z_SLOTS = 3
ATTN_SLOTS = 2
ATTN_SUB = 256
ATTN_KEY_CHUNK = 512
ATTN_MAX_SHIFTED_RANGE = 100.0
PROJ_SUB = 256


def _cparams(n_axes):
    return pltpu.CompilerParams(
        dimension_semantics=("arbitrary",) * n_axes, vmem_limit_bytes=VMEM_LIMIT)


def _const_spec(shape):
    nd = len(shape)
    return pl.BlockSpec(shape, lambda *_: (0,) * nd, pipeline_mode=pl.Buffered(1))


def _silu(x):
    return x * (1.0 / (1.0 + jnp.exp(-x)))


def _rms_mod(x, g, shift, scale):
    ms = jnp.mean(x * x, axis=-1, keepdims=True)
    y = x * lax.rsqrt(ms + EPS) * g
    return y * (1.0 + scale) + shift


def _mod_kernel(c_ref, w_ref, b_ref, o_ref):
    s = _silu(c_ref[...]).astype(BF16)
    o_ref[0] = jnp.dot(s, w_ref[0].astype(BF16), preferred_element_type=F32) + b_ref[0]


def _mod_call(cvec, w_mod, b_mod):
    depth, d, n = w_mod.shape
    rows = cvec.shape[0]
    tn = 1536
    return pl.pallas_call(
        _mod_kernel,
        grid=(depth, n // tn),
        in_specs=[
            pl.BlockSpec((rows, d), lambda i, j: (0, 0)),
            pl.BlockSpec((1, d, tn), lambda i, j: (i, 0, j)),
            pl.BlockSpec((1, 1, tn), lambda i, j: (i, 0, j)),
        ],
        out_specs=pl.BlockSpec((1, rows, tn), lambda i, j: (i, 0, j)),
        out_shape=jax.ShapeDtypeStruct((depth, rows, n), F32),
        compiler_params=_cparams(2),
        name="mod",
    )(cvec, w_mod, b_mod.reshape(depth, 1, n))


def _proj_kernel(x_ref, sh_ref, sc_ref, g_ref, w_ref, eq_ref, ek_ref, gain_ref, cos_ref, sin_ref,
                 q_ref, k_ref, vt_ref, u_ref, sv_ref, xp_ref):
    t = x_ref.shape[1]
    sub = min(t, PROJ_SUB)
    nq = N_HEADS * HEAD_DIM
    nqk = (N_HEADS + N_KV_HEADS) * HEAD_DIM
    lane = lax.broadcasted_iota(jnp.int32, (sub, LANES), 1)
    first = (lane % 32) < 16
    lo = lane < HEAD_DIM

    def project(r):
        rows = slice(sub * r, sub * (r + 1))
        h = _rms_mod(x_ref[0, rows], g_ref[...], sh_ref[0], sc_ref[0]).astype(BF16)
        return jnp.dot(h, w_ref[...], preferred_element_type=F32)

    def finish(r, p):
        rows = slice(sub * r, sub * (r + 1))
        qk = p[:, :nqk]
        sq = (qk * qk).astype(BF16)
        msh = jnp.concatenate(
            [jnp.dot(sq[:, :nq], eq_ref[...], preferred_element_type=F32),
             jnp.dot(sq[:, nq:], ek_ref[...], preferred_element_type=F32)], axis=1)
        qkn = qk * lax.rsqrt(msh + EPS) * gain_ref[...]
        c = cos_ref[rows]
        s = sin_ref[rows]
        for j in range(nqk // LANES):
            blk = qkn[:, LANES * j:LANES * (j + 1)]
            sw = jnp.where(first, pltpu.roll(blk, LANES - 16, axis=1), pltpu.roll(blk, 16, axis=1))
            rot = blk * c + sw * s
            if j < N_HEADS // 2:
                kvh = (2 * j) // (N_HEADS // N_KV_HEADS)
                rr = pltpu.roll(rot, HEAD_DIM, axis=1)
                keep = lo if kvh == 0 else jnp.logical_not(lo)
                even = rot if kvh == 0 else rr
                odd = rr if kvh == 0 else rot
                q_ref[0, rows, LANES * (2 * j):LANES * (2 * j + 1)] = (
                    jnp.where(keep, even, 0.0).astype(BF16))
                q_ref[0, rows, LANES * (2 * j + 1):LANES * (2 * j + 2)] = (
                    jnp.where(keep, odd, 0.0).astype(BF16))
            else:
                k_ref[0, rows] = rot.astype(BF16)
        vt_ref[0, :, rows] = p[:, 640:768].T.astype(BF16)
        u_ref[0, rows] = p[:, 768:1024]
        sv_ref[0, rows] = p[:, 1024:1280].astype(BF16)
        xp_ref[0, rows] = p[:, 1280:1536]

    nsub = t // sub
    p_next = project(0)
    for r in range(nsub):
        p = p_next
        if r + 1 < nsub:
            p_next = project(r + 1)
        finish(r, p)


def _proj_call(x, sh, sc, g, w_in, e_q, e_k, gain, cos_t, sin_t, tile):
    b, l, d = x.shape
    nt = l // tile
    n_in = w_in.shape[1]
    row = lambda t_, b_: (b_, t_, 0)
    vec = lambda t_, b_: (b_, 0, 0)
    tab = lambda t_, b_: (t_, 0)
    return pl.pallas_call(
        _proj_kernel,
        grid=(nt, b),
        in_specs=[
            pl.BlockSpec((1, tile, d), row),
            pl.BlockSpec((1, 1, d), vec),
            pl.BlockSpec((1, 1, d), vec),
            _const_spec((1, d)),
            _const_spec((d, n_in)),
            _const_spec(e_q.shape),
            _const_spec(e_k.shape),
            _const_spec(gain.shape),
            pl.BlockSpec((tile, LANES), tab),
            pl.BlockSpec((tile, LANES), tab),
        ],
        out_specs=[
            pl.BlockSpec((1, tile, N_HEADS * LANES), row),
            pl.BlockSpec((1, tile, LANES), row),
            pl.BlockSpec((1, LANES, tile), lambda t_, b_: (b_, 0, t_)),
            pl.BlockSpec((1, tile, 256), row),
            pl.BlockSpec((1, tile, 256), row),
            pl.BlockSpec((1, tile, 256), row),
        ],
        out_shape=[
            jax.ShapeDtypeStruct((b, l, N_HEADS * LANES), BF16),
            jax.ShapeDtypeStruct((b, l, LANES), BF16),
            jax.ShapeDtypeStruct((b, LANES, l), BF16),
            jax.ShapeDtypeStruct((b, l, 256), F32),
            jax.ShapeDtypeStruct((b, l, 256), BF16),
            jax.ShapeDtypeStruct((b, l, 256), F32),
        ],
        compiler_params=_cparams(2),
        name="proj",
    )(x, sh, sc, g, w_in, e_q, e_k, gain, cos_t, sin_t)


def _attn_kernel(*refs, n_kv):
    q_ref = refs[0]
    k_refs = refs[1:1 + n_kv]
    vt_refs = refs[1 + n_kv:1 + 2 * n_kv]
    o_ref, s_ref, p_ref = refs[1 + 2 * n_kv:]
    group = N_HEADS // N_KV_HEADS
    nslot = s_ref.shape[0]
    sub = s_ref.shape[2]
    nunit = (q_ref.shape[1] // sub) * N_HEADS

    chunks = []
    base = 0
    for j, kr in enumerate(k_refs):
        n = kr.shape[1]
        step = min(n, ATTN_KEY_CHUNK)
        chunks += [(j, r0, step, base + r0) for r0 in range(0, n, step)]
        base += n

    def q_unit(u):
        r, i = divmod(u, N_HEADS)
        return q_ref[0, sub * r:sub * (r + 1), LANES * i:LANES * (i + 1)]

    pending = []

    def emit(u, ot, l):
        r, i = divmod(u, N_HEADS)
        kvh = i // group
        pending.append(ot[HEAD_DIM * kvh:HEAD_DIM * (kvh + 1)] * (1.0 / l))
        if i % 2 == 1:
            pair = jnp.concatenate(pending, axis=0)
            pending.clear()
            o_ref[0, sub * r:sub * (r + 1), LANES * (i // 2):LANES * (i // 2 + 1)] = (
                pair.T.astype(BF16))

    m = None
    l_prev = None
    for u in range(-1, nunit + 1):
        do_scores = u + 1 < nunit
        do_softmax = 0 <= u < nunit
        do_pv = 1 <= u
        q_next = q_unit(u + 1) if do_scores else None
        mparts, lparts, ot = [], [], None
        for j, r0, n, s0 in chunks:
            if do_scores:
                st = lax.dot_general(k_refs[j][0, r0:r0 + n], q_next, (((1,), (1,)), ((), ())),
                                     preferred_element_type=F32)
                s_ref[(u + 1) % nslot, s0:s0 + n] = st
                mparts.append(jnp.max(st, axis=0, keepdims=True))
            if do_softmax:
                p = jnp.exp2(s_ref[u % nslot, s0:s0 + n] - m)
                lparts.append(jnp.sum(p, axis=0, keepdims=True))
                p_ref[u % nslot, s0:s0 + n] = p.astype(BF16)
            if do_pv:
                d = jnp.dot(vt_refs[j][0, :, r0:r0 + n], p_ref[(u - 1) % nslot, s0:s0 + n],
                            preferred_element_type=F32)
                ot = d if ot is None else ot + d
        if do_pv:
            emit(u - 1, ot, l_prev)
        if do_softmax:
            l_prev = functools.reduce(jnp.add, lparts)
        if do_scores:
            m = functools.reduce(jnp.maximum, mparts)


def _attn_bounded_kernel(*refs, n_kv):
    q_ref = refs[0]
    k_refs = refs[1:1 + n_kv]
    vt_refs = refs[1 + n_kv:1 + 2 * n_kv]
    shift_ref, o_ref, p_ref = refs[1 + 2 * n_kv:]
    group = N_HEADS // N_KV_HEADS
    nslot = p_ref.shape[0]
    sub = p_ref.shape[2]
    nunit = (q_ref.shape[1] // sub) * N_HEADS
    shift = shift_ref[0]

    chunks = []
    base = 0
    for j, kr in enumerate(k_refs):
        n = kr.shape[1]
        step = min(n, ATTN_KEY_CHUNK)
        chunks += [(j, r0, step, base + r0) for r0 in range(0, n, step)]
        base += n

    def q_unit(u):
        r, i = divmod(u, N_HEADS)
        return q_ref[0, sub * r:sub * (r + 1), LANES * i:LANES * (i + 1)]

    def v_ext(kvh, j, r0, n):
        vt = vt_refs[j][0, HEAD_DIM * kvh:HEAD_DIM * (kvh + 1), r0:r0 + n]
        return jnp.concatenate([vt, jnp.ones((2 * SUBLANES, n), BF16)], axis=0)

    pending = []

    def emit(u, ot):
        r, i = divmod(u, N_HEADS)
        pending.append(ot[:HEAD_DIM] * (1.0 / ot[HEAD_DIM:HEAD_DIM + 1]))
        if i % 2 == 1:
            pair = jnp.concatenate(pending, axis=0)
            pending.clear()
            o_ref[0, sub * r:sub * (r + 1), LANES * (i // 2):LANES * (i // 2 + 1)] = (
                pair.T.astype(BF16))

    for u in range(nunit + 1):
        q_u = q_unit(u) if u < nunit else None
        kvh_prev = ((u - 1) % N_HEADS) // group
        ot = None
        for j, r0, n, s0 in chunks:
            if u < nunit:
                st = lax.dot_general(k_refs[j][0, r0:r0 + n], q_u, (((1,), (1,)), ((), ())),
                                     preferred_element_type=F32)
                p_ref[u % nslot, s0:s0 + n] = jnp.exp2(st - shift).astype(BF16)
            if u >= 1:
                d = jnp.dot(v_ext(kvh_prev, j, r0, n), p_ref[(u - 1) % nslot, s0:s0 + n],
                            preferred_element_type=F32)
                ot = d if ot is None else ot + d
        if u >= 1:
            emit(u - 1, ot)


def _attn_call(q, ks, vts, tq, score_bound=None):
    b, s, _ = q.shape
    n_kv = len(ks)
    nk = sum(k.shape[1] for k in ks)
    sub = min(tq, ATTN_SUB)
    kv_map = lambda b_, t_: (b_, 0, 0)
    in_specs = ([pl.BlockSpec((1, tq, N_HEADS * LANES), lambda b_, t_: (b_, t_, 0))]
                + [pl.BlockSpec((1, k.shape[1], LANES), kv_map) for k in ks]
                + [pl.BlockSpec((1, LANES, vt.shape[2]), kv_map) for vt in vts])
    common = dict(
        grid=(b, s // tq),
        out_specs=pl.BlockSpec((1, tq, N_HEADS * HEAD_DIM), lambda b_, t_: (b_, t_, 0)),
        out_shape=jax.ShapeDtypeStruct((b, s, N_HEADS * HEAD_DIM), BF16),
        compiler_params=_cparams(2),
    )

    def exact(_):
        return pl.pallas_call(
            functools.partial(_attn_kernel, n_kv=n_kv),
            in_specs=in_specs,
            scratch_shapes=[pltpu.VMEM((ATTN_SLOTS, nk, sub), F32),
                            pltpu.VMEM((ATTN_SLOTS, nk, sub), BF16)],
            name="attn", **common,
        )(q, *ks, *vts)

    if score_bound is None:
        return exact(None)

    def bounded(shift):
        return pl.pallas_call(
            functools.partial(_attn_bounded_kernel, n_kv=n_kv),
            in_specs=in_specs + [pl.BlockSpec(memory_space=pltpu.SMEM)],
            scratch_shapes=[pltpu.VMEM((ATTN_SLOTS, nk, sub), BF16)],
            name="attn_bounded", **common,
        )(q, *ks, *vts, shift.reshape(1))

    return lax.cond(2.0 * score_bound <= ATTN_MAX_SHIFTED_RANGE, bounded, exact, score_bound)


def _mix_kernel(att_ref, u_ref, sv_ref, xp_ref, xpp_ref, xpn_ref, x_ref, g1_ref,
                ws_ref, bias_ref, wp_ref, ps_ref, icnt_ref, wo_ref, o_ref):
    t = x_ref.shape[1]
    ti = pl.program_id(1)
    nt = pl.num_programs(1)
    lane = lax.broadcasted_iota(jnp.int32, (CHUNK, LANES), 1)
    lo = lane < 64

    sv = sv_ref[0]
    rows = []
    for n in range(t // CHUNK):
        cols = []
        for jp in range(N_SGU_GROUPS // 2):
            blk = sv[CHUNK * n:CHUNK * (n + 1), LANES * jp:LANES * (jp + 1)]
            r0 = jnp.dot(ws_ref[2 * jp], blk, preferred_element_type=F32)
            r1 = jnp.dot(ws_ref[2 * jp + 1], blk, preferred_element_type=F32)
            cols.append(jnp.where(lo, r0, r1))
        rows.append(jnp.concatenate(cols, axis=1) + bias_ref[...])
    sgu = u_ref[0] * jnp.concatenate(rows, axis=0)

    xp = xp_ref[0]
    prev = jnp.where(ti > 0, xpp_ref[0], 0.0)
    nxt = jnp.where(ti < nt - 1, xpn_ref[0], 0.0)
    xe = jnp.concatenate([prev, xp, nxt], axis=0)
    lane_t = lax.broadcasted_iota(jnp.int32, (t, LANES), 1)
    lo_t = lane_t < 64

    def shifted(col, k_):
        return xe[HALO + k_:HALO + k_ + t, LANES * col:LANES * (col + 1)]

    def window(col, w, inner=None, inner_w=0):
        left = w // 2
        right = w - 1 - left
        ileft = inner_w // 2
        iright = inner_w - 1 - ileft
        acc = inner
        for k_ in range(-left, right + 1):
            if inner is not None and -ileft <= k_ <= iright:
                continue
            term = shifted(col, k_)
            acc = term if acc is None else acc + term
        return acc

    w2 = window(0, 2)
    w4 = window(0, 4, w2, 2)
    w8 = window(1, 8)
    w16 = window(1, 16, w8, 8)
    tot = jnp.concatenate([jnp.where(lo_t, w2, w4), jnp.where(lo_t, w8, w16)], axis=1)
    y = tot * icnt_ref[...] - xp
    pool = jnp.dot(y.astype(BF16), wp_ref[...], preferred_element_type=F32) * ps_ref[...]

    na = att_ref.shape[2]
    ns = sgu.shape[1]
    mix = jnp.dot(att_ref[0], wo_ref[0:na, :], preferred_element_type=F32)
    mix = mix + jnp.dot(sgu.astype(BF16), wo_ref[na:na + ns, :], preferred_element_type=F32)
    mix = mix + jnp.dot(pool.astype(BF16), wo_ref[na + ns:, :], preferred_element_type=F32)
    o_ref[0] = x_ref[0] + g1_ref[0] * mix


def _mix_call(att, u, sv, xp, x, g1, ws, bias, wp_bd, pscale, icnt, w_out, tile):
    b, l, d = x.shape
    nt = l // tile
    hb = tile // HALO
    nhb = l // HALO
    row = lambda b_, t_: (b_, t_, 0)
    vec = lambda b_, t_: (b_, 0, 0)
    prev = lambda b_, t_: (b_, jnp.maximum(t_ * hb - 1, 0), 0)
    nxt = lambda b_, t_: (b_, jnp.minimum((t_ + 1) * hb, nhb - 1), 0)
    return pl.pallas_call(
        _mix_kernel,
        grid=(b, nt),
        in_specs=[
            pl.BlockSpec((1, tile, att.shape[2]), row),
            pl.BlockSpec((1, tile, 256), row),
            pl.BlockSpec((1, tile, 256), row),
            pl.BlockSpec((1, tile, 256), row),
            pl.BlockSpec((1, HALO, 256), prev),
            pl.BlockSpec((1, HALO, 256), nxt),
            pl.BlockSpec((1, tile, d), row),
            pl.BlockSpec((1, 1, d), vec),
            _const_spec(ws.shape),
            _const_spec(bias.shape),
            _const_spec(wp_bd.shape),
            _const_spec(pscale.shape),
            pl.BlockSpec((tile, 256), lambda b_, t_: (t_, 0)),
            _const_spec(w_out.shape),
        ],
        out_specs=pl.BlockSpec((1, tile, d), row),
        out_shape=jax.ShapeDtypeStruct((b, l, d), F32),
        compiler_params=_cparams(2),
        name="mix",
    )(att, u, sv, xp, xp, xp, x, g1, ws, bias, wp_bd, pscale, icnt, w_out)


def _ffn_kernel(x_ref, xp_ref, xn_ref, sh_ref, sc_ref, g2_ref, ng_ref, wg_ref, wv_ref,
                cwg_ref, cwv_ref, cbg_ref, cbv_ref, wd_ref, fg_ref, o_ref, z_ref, *, final):
    t = x_ref.shape[1]
    ti = pl.program_id(1)
    nt = pl.num_programs(1)
    x = x_ref[0]
    ng = ng_ref[...]
    sh = sh_ref[0]
    sc = sc_ref[0]
    hprev = jnp.where(ti > 0, _rms_mod(xp_ref[0], ng, sh, sc), 0.0)
    hnext = jnp.where(ti < nt - 1, _rms_mod(xn_ref[0], ng, sh, sc), 0.0)
    h = jnp.concatenate([hprev, _rms_mod(x, ng, sh, sc), hnext], axis=0).astype(BF16)
    nch = wg_ref.shape[0]

    nl = wg_ref.shape[2] // LANES
    nslot = z_ref.shape[0]

    def up(c):
        slot = c % nslot
        zg = jnp.dot(h, wg_ref[c], preferred_element_type=F32)
        zv = jnp.dot(h, wv_ref[c], preferred_element_type=F32)
        for j in range(nl):
            z_ref[slot, j] = zg[:, LANES * j:LANES * (j + 1)]
            z_ref[slot, nl + j] = zv[:, LANES * j:LANES * (j + 1)]

    def conv(slot, j, cw, cb):
        return (z_ref[slot, j, HALO - 1:HALO - 1 + t] * cw[0:1]
                + z_ref[slot, j, HALO:HALO + t] * cw[1:2]
                + z_ref[slot, j, HALO + 1:HALO + 1 + t] * cw[2:3] + cb)

    acc = jnp.zeros((t, x.shape[1]), F32)
    for c in range(min(nslot - 1, nch)):
        up(c)
    for c in range(nch):
        if c + nslot - 1 < nch:
            up(c + nslot - 1)
        slot = c % nslot
        parts = []
        for j in range(nl):
            ls = slice(LANES * j, LANES * (j + 1))
            cg = conv(slot, j, cwg_ref[c][:, ls], cbg_ref[c][:, ls])
            cv = conv(slot, nl + j, cwv_ref[c][:, ls], cbv_ref[c][:, ls])
            parts.append((_silu(cg) * cv).astype(BF16))
        a = jnp.concatenate(parts, axis=1)
        acc = acc + jnp.dot(a, wd_ref[c], preferred_element_type=F32)
    y = x + g2_ref[0] * acc
    if final:
        ms = jnp.mean(y * y, axis=-1, keepdims=True)
        y = y * lax.rsqrt(ms + EPS) * fg_ref[...]
    o_ref[0] = y


def _ffn_call(x, sh, sc, g2, ng, wg, wv, cwg, cwv, cbg, cbv, wd, fg, tile, final):
    b, l, d = x.shape
    nt = l // tile
    hb = tile // HALO
    nhb = l // HALO
    row = lambda b_, t_: (b_, t_, 0)
    vec = lambda b_, t_: (b_, 0, 0)
    prev = lambda b_, t_: (b_, jnp.maximum(t_ * hb - 1, 0), 0)
    nxt = lambda b_, t_: (b_, jnp.minimum((t_ + 1) * hb, nhb - 1), 0)
    return pl.pallas_call(
        functools.partial(_ffn_kernel, final=final),
        grid=(b, nt),
        in_specs=[
            pl.BlockSpec((1, tile, d), row),
            pl.BlockSpec((1, HALO, d), prev),
            pl.BlockSpec((1, HALO, d), nxt),
            pl.BlockSpec((1, 1, d), vec),
            pl.BlockSpec((1, 1, d), vec),
            pl.BlockSpec((1, 1, d), vec),
            _const_spec(ng.shape),
            _const_spec(wg.shape),
            _const_spec(wv.shape),
            _const_spec(cwg.shape),
            _const_spec(cwv.shape),
            _const_spec(cbg.shape),
            _const_spec(cbv.shape),
            _const_spec(wd.shape),
            _const_spec(fg.shape),
        ],
        out_specs=pl.BlockSpec((1, tile, d), row),
        out_shape=jax.ShapeDtypeStruct((b, l, d), F32),
        scratch_shapes=[pltpu.VMEM((FFN_Z_SLOTS, 2 * wg.shape[2] // LANES, tile + 2 * HALO, LANES), F32)],
        compiler_params=_cparams(2),
        name="ffn_final" if final else "ffn",
    )(x, x, x, sh, sc, g2, ng, wg, wv, cwg, cwv, cbg, cbv, wd, fg)


def _rope_tables(n):
    pos = jnp.arange(n)
    row = (pos // GRID_W).astype(F32)
    col = (pos % GRID_W).astype(F32)
    axis_dim = HEAD_DIM // 2
    inv = ROPE_BASE ** (-jnp.arange(0, axis_dim, 2, dtype=F32) / axis_dim)
    ang_r = row[:, None] * inv[None, :]
    ang_c = col[:, None] * inv[None, :]
    cr, sr, cc, sn = jnp.cos(ang_r), jnp.sin(ang_r), jnp.cos(ang_c), jnp.sin(ang_c)
    cos_h = jnp.concatenate([cr, cr, cc, cc], axis=1)
    sin_h = jnp.concatenate([-sr, sr, -sn, sn], axis=1)
    return jnp.tile(cos_h, (1, 2)), jnp.tile(sin_h, (1, 2))


def _pool_inv_count(l):
    t = np.arange(l)
    cols = []
    for w in POOL_WINDOWS:
        left = w // 2
        right = w - 1 - left
        cnt = (np.minimum(t + right, l - 1) - np.maximum(t - left, 0) + 1).astype(np.float32)
        cols.append(np.repeat((np.float32(1.0) / cnt)[:, None], POOL_GROUP_DIM, axis=1))
    return jnp.asarray(np.concatenate(cols, axis=1))


def _block_diag(w):
    g, a, b = w.shape
    out = jnp.zeros((g * a, g * b), w.dtype)
    for i in range(g):
        out = out.at[i * a:(i + 1) * a, i * b:(i + 1) * b].set(w[i])
    return out


def _head_mean_matrix(n_heads):
    n = n_heads * HEAD_DIM
    idx = np.arange(n) // HEAD_DIM
    return jnp.asarray((idx[:, None] == idx[None, :]).astype(np.float32) / HEAD_DIM, dtype=BF16)


def _split6(m):
    return [m[..., i * (m.shape[-1] // 6):(i + 1) * (m.shape[-1] // 6)] for i in range(6)]


def kernel(x, c, ctx, c_ctx, w_mod, b_mod, norm1_g, w_in, q_gain, k_gain, w_s, b_s, w_pool,
           pool_scale, w_out, norm2_g, w_up, conv_w, conv_b, w_down, final_g):
    b, s, d = x.shape
    cl = ctx.shape[1]
    depth = w_mod.shape[0]
    d_ff = w_down.shape[1]
    fc = 256
    nch = d_ff // fc

    rows = ((b + 1 + SUBLANES - 1) // SUBLANES) * SUBLANES
    cvec = jnp.zeros((rows, d), F32).at[:b].set(c).at[b].set(c_ctx)
    mod = _mod_call(cvec, w_mod, b_mod)

    cos_x, sin_x = _rope_tables(s)
    cos_c = jnp.ones((cl, LANES), F32)
    sin_c = jnp.zeros((cl, LANES), F32)
    e_q = _head_mean_matrix(N_HEADS)
    e_k = _head_mean_matrix(N_KV_HEADS)
    icnt_x = _pool_inv_count(s)
    icnt_c = _pool_inv_count(cl)
    fg = final_g.reshape(1, d)

    tile_c = min(256, cl)
    big_x = min(512, s)

    for i in range(depth):
        last = i == depth - 1
        mx = [m.reshape(b, 1, d) for m in _split6(mod[i, :b])]
        mc = [jnp.broadcast_to(m.reshape(1, 1, d), (b, 1, d)) for m in _split6(mod[i, b])]
        g1n = norm1_g[i].reshape(1, d)
        g2n = norm2_g[i].reshape(1, d)
        w_in_b = w_in[i].astype(BF16)
        w_out_b = w_out[i].astype(BF16)
        gain = jnp.concatenate([jnp.tile(q_gain[i] * (HEAD_DIM ** -0.5 * math.log2(math.e)), N_HEADS),
                                jnp.tile(k_gain[i], N_KV_HEADS)]).reshape(1, -1)
        ws_b = w_s[i].astype(BF16)
        bias = jnp.repeat(b_s[i].T, d // 4 // N_SGU_GROUPS, axis=1)
        wp_bd = _block_diag(w_pool[i]).astype(BF16)
        pscale = pool_scale[i].reshape(1, -1)
        wup = w_up[i].astype(BF16)
        wg = wup[:, :d_ff].reshape(d, nch, fc).transpose(1, 0, 2)
        wv = wup[:, d_ff:].reshape(d, nch, fc).transpose(1, 0, 2)
        cwg = conv_w[i][:, :d_ff].reshape(3, nch, fc).transpose(1, 0, 2)
        cwv = conv_w[i][:, d_ff:].reshape(3, nch, fc).transpose(1, 0, 2)
        cbg = conv_b[i][:d_ff].reshape(nch, 1, fc)
        cbv = conv_b[i][d_ff:].reshape(nch, 1, fc)
        wd = w_down[i].astype(BF16).reshape(nch, fc, d)

        qx, kx, vtx, ux, svx, xpx = _proj_call(x, mx[0], mx[1], g1n, w_in_b, e_q, e_k, gain,
                                               cos_x, sin_x, big_x)
        qc, kc, vtc, uc, svc, xpc = _proj_call(ctx, mc[0], mc[1], g1n, w_in_b, e_q, e_k, gain,
                                               cos_c, sin_c, tile_c)
        score_bound = (1.01 * HEAD_DIM * jnp.max(jnp.abs(gain[0, :N_HEADS * HEAD_DIM]))
                       * jnp.max(jnp.abs(k_gain[i])))
        att_x = _attn_call(qx, [kx, kc], [vtx, vtc], big_x, score_bound)
        x = _mix_call(att_x, ux, svx, xpx, x, mx[2], ws_b, bias, wp_bd, pscale, icnt_x,
                      w_out_b, big_x)
        x = _ffn_call(x, mx[3], mx[4], mx[5], g2n, wg, wv, cwg, cwv, cbg, cbv, wd, fg,
                      big_x, last)
        if not last:
            att_c = _attn_call(qc, [kc], [vtc], tile_c)
            ctx = _mix_call(att_c, uc, svc, xpc, ctx, mc[2], ws_b, bias, wp_bd, pscale, icnt_c,
                            w_out_b, tile_c)
            ctx = _ffn_call(ctx, mc[3], mc[4], mc[5], g2n, wg, wv, cwg, cwv, cbg, cbv, wd, fg,
                            tile_c, False)
    return x
```

```python
import functools
import math

import numpy as np
import jax
import jax.numpy as jnp
from jax import lax
from jax.experimental import pallas as pl
from jax.experimental.pallas import tpu as pltpu

F32 = jnp.float32
BF16 = jnp.bfloat16

EPS = 1e-6
GRID_W = 64
HEAD_DIM = 64
N_HEADS = 8
N_KV_HEADS = 2
ROPE_BASE = 10000.0
CHUNK = 128
N_SGU_GROUPS = 4
POOL_WINDOWS = (2, 4, 8, 16)
POOL_GROUP_DIM = 64
LANES = 128
SUBLANES = 8
HALO = SUBLANES
VMEM_LIMIT = 56 * 1024 * 1024
FFN_Z_SLOTS = 3
ATTN_SLOTS = 2
ATTN_SUB = 256
ATTN_KEY_CHUNK = 1024
ATTN_MAX_SHIFTED_RANGE = 100.0
PROJ_SUB = 256


def _cparams(n_axes):
    return pltpu.CompilerParams(
        dimension_semantics=("arbitrary",) * n_axes, vmem_limit_bytes=VMEM_LIMIT)


def _const_spec(shape):
    nd = len(shape)
    return pl.BlockSpec(shape, lambda *_: (0,) * nd, pipeline_mode=pl.Buffered(1))


def _silu(x):
    return x * (1.0 / (1.0 + jnp.exp(-x)))


def _rms_mod(x, g, shift, scale):
    ms = jnp.mean(x * x, axis=-1, keepdims=True)
    y = x * lax.rsqrt(ms + EPS) * g
    return y * (1.0 + scale) + shift


def _mod_kernel(c_ref, w_ref, b_ref, o_ref):
    s = _silu(c_ref[...]).astype(BF16)
    o_ref[0] = jnp.dot(s, w_ref[0].astype(BF16), preferred_element_type=F32) + b_ref[0]


def _mod_call(cvec, w_mod, b_mod):
    depth, d, n = w_mod.shape
    rows = cvec.shape[0]
    tn = 1536
    return pl.pallas_call(
        _mod_kernel,
        grid=(depth, n // tn),
        in_specs=[
            pl.BlockSpec((rows, d), lambda i, j: (0, 0)),
            pl.BlockSpec((1, d, tn), lambda i, j: (i, 0, j)),
            pl.BlockSpec((1, 1, tn), lambda i, j: (i, 0, j)),
        ],
        out_specs=pl.BlockSpec((1, rows, tn), lambda i, j: (i, 0, j)),
        out_shape=jax.ShapeDtypeStruct((depth, rows, n), F32),
        compiler_params=_cparams(2),
        name="mod",
    )(cvec, w_mod, b_mod.reshape(depth, 1, n))


def _proj_kernel(x_ref, sh_ref, sc_ref, g_ref, w_ref, eq_ref, ek_ref, gain_ref, cos_ref, sin_ref,
                 *out_refs, kv_only):
    if kv_only:
        k_ref, vt_ref = out_refs
    else:
        q_ref, k_ref, vt_ref, u_ref, sv_ref, xp_ref = out_refs
    t = x_ref.shape[1]
    sub = min(t, PROJ_SUB)
    nq = N_HEADS * HEAD_DIM
    nqk = (N_HEADS + N_KV_HEADS) * HEAD_DIM
    nv = N_KV_HEADS * HEAD_DIM
    col0 = nq if kv_only else 0
    lane = lax.broadcasted_iota(jnp.int32, (sub, LANES), 1)
    first = (lane % 32) < 16
    lo = lane < HEAD_DIM

    def project(r):
        rows = slice(sub * r, sub * (r + 1))
        h = _rms_mod(x_ref[0, rows], g_ref[...], sh_ref[0], sc_ref[0]).astype(BF16)
        w = w_ref[:, col0:nqk + nv] if kv_only else w_ref[...]
        return jnp.dot(h, w, preferred_element_type=F32)

    def finish(r, p):
        rows = slice(sub * r, sub * (r + 1))
        qk = p[:, :nqk - col0]
        sq = (qk * qk).astype(BF16)
        msk = jnp.dot(sq[:, nq - col0:], ek_ref[...], preferred_element_type=F32)
        if kv_only:
            msh = msk
        else:
            msh = jnp.concatenate(
                [jnp.dot(sq[:, :nq], eq_ref[...], preferred_element_type=F32), msk], axis=1)
        qkn = qk * lax.rsqrt(msh + EPS) * gain_ref[:, col0:]
        c = cos_ref[rows]
        s = sin_ref[rows]
        for j in range(col0 // LANES, nqk // LANES):
            blk = qkn[:, LANES * j - col0:LANES * (j + 1) - col0]
            sw = jnp.where(first, pltpu.roll(blk, LANES - 16, axis=1), pltpu.roll(blk, 16, axis=1))
            rot = blk * c + sw * s
            if j < N_HEADS // 2:
                kvh = (2 * j) // (N_HEADS // N_KV_HEADS)
                rr = pltpu.roll(rot, HEAD_DIM, axis=1)
                keep = lo if kvh == 0 else jnp.logical_not(lo)
                even = rot if kvh == 0 else rr
                odd = rr if kvh == 0 else rot
                q_ref[0, rows, LANES * (2 * j):LANES * (2 * j + 1)] = (
                    jnp.where(keep, even, 0.0).astype(BF16))
                q_ref[0, rows, LANES * (2 * j + 1):LANES * (2 * j + 2)] = (
                    jnp.where(keep, odd, 0.0).astype(BF16))
            else:
                k_ref[0, rows] = rot.astype(BF16)
        vt_ref[0, :, rows] = p[:, nqk - col0:nqk - col0 + nv].T.astype(BF16)
        if not kv_only:
            u_ref[0, rows] = p[:, 768:1024]
            sv_ref[0, rows] = p[:, 1024:1280].astype(BF16)
            xp_ref[0, rows] = p[:, 1280:1536]

    nsub = t // sub
    p_next = project(0)
    for r in range(nsub):
        p = p_next
        if r + 1 < nsub:
            p_next = project(r + 1)
        finish(r, p)


def _proj_call(x, sh, sc, g, w_in, e_q, e_k, gain, cos_t, sin_t, tile, kv_only=False):
    b, l, d = x.shape
    nt = l // tile
    n_in = w_in.shape[1]
    row = lambda t_, b_: (b_, t_, 0)
    vec = lambda t_, b_: (b_, 0, 0)
    tab = lambda t_, b_: (t_, 0)
    out_specs = [
        pl.BlockSpec((1, tile, N_HEADS * LANES), row),
        pl.BlockSpec((1, tile, LANES), row),
        pl.BlockSpec((1, LANES, tile), lambda t_, b_: (b_, 0, t_)),
        pl.BlockSpec((1, tile, 256), row),
        pl.BlockSpec((1, tile, 256), row),
        pl.BlockSpec((1, tile, 256), row),
    ]
    out_shape = [
        jax.ShapeDtypeStruct((b, l, N_HEADS * LANES), BF16),
        jax.ShapeDtypeStruct((b, l, LANES), BF16),
        jax.ShapeDtypeStruct((b, LANES, l), BF16),
        jax.ShapeDtypeStruct((b, l, 256), F32),
        jax.ShapeDtypeStruct((b, l, 256), BF16),
        jax.ShapeDtypeStruct((b, l, 256), F32),
    ]
    if kv_only:
        out_specs, out_shape = out_specs[1:3], out_shape[1:3]
    return pl.pallas_call(
        functools.partial(_proj_kernel, kv_only=kv_only),
        grid=(nt, b),
        in_specs=[
            pl.BlockSpec((1, tile, d), row),
            pl.BlockSpec((1, 1, d), vec),
            pl.BlockSpec((1, 1, d), vec),
            _const_spec((1, d)),
            _const_spec((d, n_in)),
            _const_spec(e_q.shape),
            _const_spec(e_k.shape),
            _const_spec(gain.shape),
            pl.BlockSpec((tile, LANES), tab),
            pl.BlockSpec((tile, LANES), tab),
        ],
        out_specs=out_specs,
        out_shape=out_shape,
        compiler_params=_cparams(2),
        name="proj_kv" if kv_only else "proj",
    )(x, sh, sc, g, w_in, e_q, e_k, gain, cos_t, sin_t)


def _attn_kernel(*refs, n_kv):
    q_ref = refs[0]
    k_refs = refs[1:1 + n_kv]
    vt_refs = refs[1 + n_kv:1 + 2 * n_kv]
    o_ref, s_ref, p_ref = refs[1 + 2 * n_kv:]
    group = N_HEADS // N_KV_HEADS
    nslot = s_ref.shape[0]
    sub = s_ref.shape[2]
    nunit = (q_ref.shape[1] // sub) * N_HEADS

    chunks = []
    base = 0
    for j, kr in enumerate(k_refs):
        n = kr.shape[1]
        step = min(n, ATTN_KEY_CHUNK)
        chunks += [(j, r0, step, base + r0) for r0 in range(0, n, step)]
        base += n

    def q_unit(u):
        r, i = divmod(u, N_HEADS)
        return q_ref[0, sub * r:sub * (r + 1), LANES * i:LANES * (i + 1)]

    pending = []

    def emit(u, ot, l):
        r, i = divmod(u, N_HEADS)
        kvh = i // group
        pending.append(ot[HEAD_DIM * kvh:HEAD_DIM * (kvh + 1)] * (1.0 / l))
        if i % 2 == 1:
            pair = jnp.concatenate(pending, axis=0)
            pending.clear()
            o_ref[0, sub * r:sub * (r + 1), LANES * (i // 2):LANES * (i // 2 + 1)] = (
                pair.T.astype(BF16))

    m = None
    l_prev = None
    for u in range(-1, nunit + 1):
        do_scores = u + 1 < nunit
        do_softmax = 0 <= u < nunit
        do_pv = 1 <= u
        q_next = q_unit(u + 1) if do_scores else None
        mparts, lparts, ot = [], [], None
        for j, r0, n, s0 in chunks:
            if do_scores:
                st = lax.dot_general(k_refs[j][0, r0:r0 + n], q_next, (((1,), (1,)), ((), ())),
                                     preferred_element_type=F32)
                s_ref[(u + 1) % nslot, s0:s0 + n] = st
                mparts.append(jnp.max(st, axis=0, keepdims=True))
            if do_softmax:
                p = jnp.exp2(s_ref[u % nslot, s0:s0 + n] - m)
                lparts.append(jnp.sum(p, axis=0, keepdims=True))
                p_ref[u % nslot, s0:s0 + n] = p.astype(BF16)
            if do_pv:
                d = jnp.dot(vt_refs[j][0, :, r0:r0 + n], p_ref[(u - 1) % nslot, s0:s0 + n],
                            preferred_element_type=F32)
                ot = d if ot is None else ot + d
        if do_pv:
            emit(u - 1, ot, l_prev)
        if do_softmax:
            l_prev = functools.reduce(jnp.add, lparts)
        if do_scores:
            m = functools.reduce(jnp.maximum, mparts)


def _attn_bounded_kernel(*refs, n_kv):
    q_ref = refs[0]
    k_refs = refs[1:1 + n_kv]
    vt_refs = refs[1 + n_kv:1 + 2 * n_kv]
    shift_ref, o_ref, p_ref = refs[1 + 2 * n_kv:]
    group = N_HEADS // N_KV_HEADS
    nslot = p_ref.shape[0]
    sub = p_ref.shape[2]
    nunit = (q_ref.shape[1] // sub) * N_HEADS
    shift = shift_ref[0]

    chunks = []
    base = 0
    for j, kr in enumerate(k_refs):
        n = kr.shape[1]
        step = min(n, ATTN_KEY_CHUNK)
        chunks += [(j, r0, step, base + r0) for r0 in range(0, n, step)]
        base += n

    def q_unit(u):
        r, i = divmod(u, N_HEADS)
        return q_ref[0, sub * r:sub * (r + 1), LANES * i:LANES * (i + 1)]

    def v_ext(kvh, j, r0, n):
        vt = vt_refs[j][0, HEAD_DIM * kvh:HEAD_DIM * (kvh + 1), r0:r0 + n]
        return jnp.concatenate([vt, jnp.ones((2 * SUBLANES, n), BF16)], axis=0)

    pending = []

    def emit(u, ot):
        r, i = divmod(u, N_HEADS)
        pending.append(ot[:HEAD_DIM] * (1.0 / ot[HEAD_DIM:HEAD_DIM + 1]))
        if i % 2 == 1:
            pair = jnp.concatenate(pending, axis=0)
            pending.clear()
            o_ref[0, sub * r:sub * (r + 1), LANES * (i // 2):LANES * (i // 2 + 1)] = (
                pair.T.astype(BF16))

    for u in range(nunit + 1):
        q_u = q_unit(u) if u < nunit else None
        kvh_prev = ((u - 1) % N_HEADS) // group
        ot = None
        for j, r0, n, s0 in chunks:
            if u < nunit:
                st = lax.dot_general(k_refs[j][0, r0:r0 + n], q_u, (((1,), (1,)), ((), ())),
                                     preferred_element_type=F32)
                p_ref[u % nslot, s0:s0 + n] = jnp.exp2(st - shift).astype(BF16)
            if u >= 1:
                d = jnp.dot(v_ext(kvh_prev, j, r0, n), p_ref[(u - 1) % nslot, s0:s0 + n],
                            preferred_element_type=F32)
                ot = d if ot is None else ot + d
        if u >= 1:
            emit(u - 1, ot)


def _attn_call(q, ks, vts, tq, score_bound=None):
    b, s, _ = q.shape
    n_kv = len(ks)
    nk = sum(k.shape[1] for k in ks)
    sub = min(tq, ATTN_SUB)
    kv_map = lambda b_, t_: (b_, 0, 0)
    in_specs = ([pl.BlockSpec((1, tq, N_HEADS * LANES), lambda b_, t_: (b_, t_, 0))]
                + [pl.BlockSpec((1, k.shape[1], LANES), kv_map) for k in ks]
                + [pl.BlockSpec((1, LANES, vt.shape[2]), kv_map) for vt in vts])
    common = dict(
        grid=(b, s // tq),
        out_specs=pl.BlockSpec((1, tq, N_HEADS * HEAD_DIM), lambda b_, t_: (b_, t_, 0)),
        out_shape=jax.ShapeDtypeStruct((b, s, N_HEADS * HEAD_DIM), BF16),
        compiler_params=_cparams(2),
    )

    def exact(_):
        return pl.pallas_call(
            functools.partial(_attn_kernel, n_kv=n_kv),
            in_specs=in_specs,
            scratch_shapes=[pltpu.VMEM((ATTN_SLOTS, nk, sub), F32),
                            pltpu.VMEM((ATTN_SLOTS, nk, sub), BF16)],
            name="attn", **common,
        )(q, *ks, *vts)

    if score_bound is None:
        return exact(None)

    def bounded(shift):
        return pl.pallas_call(
            functools.partial(_attn_bounded_kernel, n_kv=n_kv),
            in_specs=in_specs + [pl.BlockSpec(memory_space=pltpu.SMEM)],
            scratch_shapes=[pltpu.VMEM((ATTN_SLOTS, nk, sub), BF16)],
            name="attn_bounded", **common,
        )(q, *ks, *vts, shift.reshape(1))

    return lax.cond(2.0 * score_bound <= ATTN_MAX_SHIFTED_RANGE, bounded, exact, score_bound)


def _mix_kernel(att_ref, u_ref, sv_ref, xp_ref, xpp_ref, xpn_ref, x_ref, g1_ref,
                ws_ref, bias_ref, wp_ref, ps_ref, icnt_ref, wo_ref, o_ref):
    t = x_ref.shape[1]
    ti = pl.program_id(1)
    nt = pl.num_programs(1)
    lane = lax.broadcasted_iota(jnp.int32, (CHUNK, LANES), 1)
    lo = lane < 64

    sv = sv_ref[0]
    rows = []
    for n in range(t // CHUNK):
        cols = []
        for jp in range(N_SGU_GROUPS // 2):
            blk = sv[CHUNK * n:CHUNK * (n + 1), LANES * jp:LANES * (jp + 1)]
            r0 = jnp.dot(ws_ref[2 * jp], blk, preferred_element_type=F32)
            r1 = jnp.dot(ws_ref[2 * jp + 1], blk, preferred_element_type=F32)
            cols.append(jnp.where(lo, r0, r1))
        rows.append(jnp.concatenate(cols, axis=1) + bias_ref[...])
    sgu = u_ref[0] * jnp.concatenate(rows, axis=0)

    xp = xp_ref[0]
    prev = jnp.where(ti > 0, xpp_ref[0], 0.0)
    nxt = jnp.where(ti < nt - 1, xpn_ref[0], 0.0)
    xe = jnp.concatenate([prev, xp, nxt], axis=0)
    lane_t = lax.broadcasted_iota(jnp.int32, (t, LANES), 1)
    lo_t = lane_t < 64

    def shifted(col, k_):
        return xe[HALO + k_:HALO + k_ + t, LANES * col:LANES * (col + 1)]

    def window(col, w, inner=None, inner_w=0):
        left = w // 2
        right = w - 1 - left
        ileft = inner_w // 2
        iright = inner_w - 1 - ileft
        acc = inner
        for k_ in range(-left, right + 1):
            if inner is not None and -ileft <= k_ <= iright:
                continue
            term = shifted(col, k_)
            acc = term if acc is None else acc + term
        return acc

    w2 = window(0, 2)
    w4 = window(0, 4, w2, 2)
    w8 = window(1, 8)
    w16 = window(1, 16, w8, 8)
    tot = jnp.concatenate([jnp.where(lo_t, w2, w4), jnp.where(lo_t, w8, w16)], axis=1)
    y = tot * icnt_ref[...] - xp
    pool = jnp.dot(y.astype(BF16), wp_ref[...], preferred_element_type=F32) * ps_ref[...]

    na = att_ref.shape[2]
    ns = sgu.shape[1]
    mix = jnp.dot(att_ref[0], wo_ref[0:na, :], preferred_element_type=F32)
    mix = mix + jnp.dot(sgu.astype(BF16), wo_ref[na:na + ns, :], preferred_element_type=F32)
    mix = mix + jnp.dot(pool.astype(BF16), wo_ref[na + ns:, :], preferred_element_type=F32)
    o_ref[0] = x_ref[0] + g1_ref[0] * mix


def _mix_call(att, u, sv, xp, x, g1, ws, bias, wp_bd, pscale, icnt, w_out, tile):
    b, l, d = x.shape
    nt = l // tile
    hb = tile // HALO
    nhb = l // HALO
    row = lambda b_, t_: (b_, t_, 0)
    vec = lambda b_, t_: (b_, 0, 0)
    prev = lambda b_, t_: (b_, jnp.maximum(t_ * hb - 1, 0), 0)
    nxt = lambda b_, t_: (b_, jnp.minimum((t_ + 1) * hb, nhb - 1), 0)
    return pl.pallas_call(
        _mix_kernel,
        grid=(b, nt),
        in_specs=[
            pl.BlockSpec((1, tile, att.shape[2]), row),
            pl.BlockSpec((1, tile, 256), row),
            pl.BlockSpec((1, tile, 256), row),
            pl.BlockSpec((1, tile, 256), row),
            pl.BlockSpec((1, HALO, 256), prev),
            pl.BlockSpec((1, HALO, 256), nxt),
            pl.BlockSpec((1, tile, d), row),
            pl.BlockSpec((1, 1, d), vec),
            _const_spec(ws.shape),
            _const_spec(bias.shape),
            _const_spec(wp_bd.shape),
            _const_spec(pscale.shape),
            pl.BlockSpec((tile, 256), lambda b_, t_: (t_, 0)),
            _const_spec(w_out.shape),
        ],
        out_specs=pl.BlockSpec((1, tile, d), row),
        out_shape=jax.ShapeDtypeStruct((b, l, d), F32),
        compiler_params=_cparams(2),
        name="mix",
    )(att, u, sv, xp, xp, xp, x, g1, ws, bias, wp_bd, pscale, icnt, w_out)


def _ffn_kernel(x_ref, xp_ref, xn_ref, sh_ref, sc_ref, g2_ref, ng_ref, wg_ref, wv_ref,
                cwg_ref, cwv_ref, cbg_ref, cbv_ref, wd_ref, fg_ref, o_ref, z_ref, *, final):
    t = x_ref.shape[1]
    ti = pl.program_id(1)
    nt = pl.num_programs(1)
    x = x_ref[0]
    ng = ng_ref[...]
    sh = sh_ref[0]
    sc = sc_ref[0]
    hprev = jnp.where(ti > 0, _rms_mod(xp_ref[0], ng, sh, sc), 0.0)
    hnext = jnp.where(ti < nt - 1, _rms_mod(xn_ref[0], ng, sh, sc), 0.0)
    h = jnp.concatenate([hprev, _rms_mod(x, ng, sh, sc), hnext], axis=0).astype(BF16)
    nch = wg_ref.shape[0]

    nl = wg_ref.shape[2] // LANES
    nslot = z_ref.shape[0]

    def up(c):
        slot = c % nslot
        zg = jnp.dot(h, wg_ref[c], preferred_element_type=F32)
        zv = jnp.dot(h, wv_ref[c], preferred_element_type=F32)
        for j in range(nl):
            z_ref[slot, j] = zg[:, LANES * j:LANES * (j + 1)]
            z_ref[slot, nl + j] = zv[:, LANES * j:LANES * (j + 1)]

    def conv(slot, j, cw, cb):
        return (z_ref[slot, j, HALO - 1:HALO - 1 + t] * cw[0:1]
                + z_ref[slot, j, HALO:HALO + t] * cw[1:2]
                + z_ref[slot, j, HALO + 1:HALO + 1 + t] * cw[2:3] + cb)

    acc = jnp.zeros((t, x.shape[1]), F32)
    for c in range(min(nslot - 1, nch)):
        up(c)
    for c in range(nch):
        if c + nslot - 1 < nch:
            up(c + nslot - 1)
        slot = c % nslot
        parts = []
        for j in range(nl):
            ls = slice(LANES * j, LANES * (j + 1))
            cg = conv(slot, j, cwg_ref[c][:, ls], cbg_ref[c][:, ls])
            cv = conv(slot, nl + j, cwv_ref[c][:, ls], cbv_ref[c][:, ls])
            parts.append((_silu(cg) * cv).astype(BF16))
        a = jnp.concatenate(parts, axis=1)
        acc = acc + jnp.dot(a, wd_ref[c], preferred_element_type=F32)
    y = x + g2_ref[0] * acc
    if final:
        ms = jnp.mean(y * y, axis=-1, keepdims=True)
        y = y * lax.rsqrt(ms + EPS) * fg_ref[...]
    o_ref[0] = y


def _ffn_call(x, sh, sc, g2, ng, wg, wv, cwg, cwv, cbg, cbv, wd, fg, tile, final):
    b, l, d = x.shape
    nt = l // tile
    hb = tile // HALO
    nhb = l // HALO
    row = lambda b_, t_: (b_, t_, 0)
    vec = lambda b_, t_: (b_, 0, 0)
    prev = lambda b_, t_: (b_, jnp.maximum(t_ * hb - 1, 0), 0)
    nxt = lambda b_, t_: (b_, jnp.minimum((t_ + 1) * hb, nhb - 1), 0)
    return pl.pallas_call(
        functools.partial(_ffn_kernel, final=final),
        grid=(b, nt),
        in_specs=[
            pl.BlockSpec((1, tile, d), row),
            pl.BlockSpec((1, HALO, d), prev),
            pl.BlockSpec((1, HALO, d), nxt),
            pl.BlockSpec((1, 1, d), vec),
            pl.BlockSpec((1, 1, d), vec),
            pl.BlockSpec((1, 1, d), vec),
            _const_spec(ng.shape),
            _const_spec(wg.shape),
            _const_spec(wv.shape),
            _const_spec(cwg.shape),
            _const_spec(cwv.shape),
            _const_spec(cbg.shape),
            _const_spec(cbv.shape),
            _const_spec(wd.shape),
            _const_spec(fg.shape),
        ],
        out_specs=pl.BlockSpec((1, tile, d), row),
        out_shape=jax.ShapeDtypeStruct((b, l, d), F32),
        scratch_shapes=[pltpu.VMEM((FFN_Z_SLOTS, 2 * wg.shape[2] // LANES, tile + 2 * HALO, LANES), F32)],
        compiler_params=_cparams(2),
        name="ffn_final" if final else "ffn",
    )(x, x, x, sh, sc, g2, ng, wg, wv, cwg, cwv, cbg, cbv, wd, fg)


def _rope_tables(n):
    pos = jnp.arange(n)
    row = (pos // GRID_W).astype(F32)
    col = (pos % GRID_W).astype(F32)
    axis_dim = HEAD_DIM // 2
    inv = ROPE_BASE ** (-jnp.arange(0, axis_dim, 2, dtype=F32) / axis_dim)
    ang_r = row[:, None] * inv[None, :]
    ang_c = col[:, None] * inv[None, :]
    cr, sr, cc, sn = jnp.cos(ang_r), jnp.sin(ang_r), jnp.cos(ang_c), jnp.sin(ang_c)
    cos_h = jnp.concatenate([cr, cr, cc, cc], axis=1)
    sin_h = jnp.concatenate([-sr, sr, -sn, sn], axis=1)
    return jnp.tile(cos_h, (1, 2)), jnp.tile(sin_h, (1, 2))


def _pool_inv_count(l):
    t = np.arange(l)
    cols = []
    for w in POOL_WINDOWS:
        left = w // 2
        right = w - 1 - left
        cnt = (np.minimum(t + right, l - 1) - np.maximum(t - left, 0) + 1).astype(np.float32)
        cols.append(np.repeat((np.float32(1.0) / cnt)[:, None], POOL_GROUP_DIM, axis=1))
    return jnp.asarray(np.concatenate(cols, axis=1))


def _block_diag(w):
    g, a, b = w.shape
    out = jnp.zeros((g * a, g * b), w.dtype)
    for i in range(g):
        out = out.at[i * a:(i + 1) * a, i * b:(i + 1) * b].set(w[i])
    return out


def _head_mean_matrix(n_heads):
    n = n_heads * HEAD_DIM
    idx = np.arange(n) // HEAD_DIM
    return jnp.asarray((idx[:, None] == idx[None, :]).astype(np.float32) / HEAD_DIM, dtype=BF16)


def _split6(m):
    return [m[..., i * (m.shape[-1] // 6):(i + 1) * (m.shape[-1] // 6)] for i in range(6)]


def kernel(x, c, ctx, c_ctx, w_mod, b_mod, norm1_g, w_in, q_gain, k_gain, w_s, b_s, w_pool,
           pool_scale, w_out, norm2_g, w_up, conv_w, conv_b, w_down, final_g):
    b, s, d = x.shape
    cl = ctx.shape[1]
    depth = w_mod.shape[0]
    d_ff = w_down.shape[1]
    fc = 256
    nch = d_ff // fc

    rows = ((b + 1 + SUBLANES - 1) // SUBLANES) * SUBLANES
    cvec = jnp.zeros((rows, d), F32).at[:b].set(c).at[b].set(c_ctx)
    mod = _mod_call(cvec, w_mod, b_mod)

    cos_x, sin_x = _rope_tables(s)
    cos_c = jnp.ones((cl, LANES), F32)
    sin_c = jnp.zeros((cl, LANES), F32)
    e_q = _head_mean_matrix(N_HEADS)
    e_k = _head_mean_matrix(N_KV_HEADS)
    icnt_x = _pool_inv_count(s)
    icnt_c = _pool_inv_count(cl)
    fg = final_g.reshape(1, d)

    tile_c = min(256, cl)
    big_x = min(512, s)
    proj_x = min(1024, s)

    for i in range(depth):
        last = i == depth - 1
        mx = [m.reshape(b, 1, d) for m in _split6(mod[i, :b])]
        mc = [jnp.broadcast_to(m.reshape(1, 1, d), (b, 1, d)) for m in _split6(mod[i, b])]
        g1n = norm1_g[i].reshape(1, d)
        g2n = norm2_g[i].reshape(1, d)
        w_in_b = w_in[i].astype(BF16)
        w_out_b = w_out[i].astype(BF16)
        gain = jnp.concatenate([jnp.tile(q_gain[i] * (HEAD_DIM ** -0.5 * math.log2(math.e)), N_HEADS),
                                jnp.tile(k_gain[i], N_KV_HEADS)]).reshape(1, -1)
        ws_b = w_s[i].astype(BF16)
        bias = jnp.repeat(b_s[i].T, d // 4 // N_SGU_GROUPS, axis=1)
        wp_bd = _block_diag(w_pool[i]).astype(BF16)
        pscale = pool_scale[i].reshape(1, -1)
        wup = w_up[i].astype(BF16)
        wg = wup[:, :d_ff].reshape(d, nch, fc).transpose(1, 0, 2)
        wv = wup[:, d_ff:].reshape(d, nch, fc).transpose(1, 0, 2)
        cwg = conv_w[i][:, :d_ff].reshape(3, nch, fc).transpose(1, 0, 2)
        cwv = conv_w[i][:, d_ff:].reshape(3, nch, fc).transpose(1, 0, 2)
        cbg = conv_b[i][:d_ff].reshape(nch, 1, fc)
        cbv = conv_b[i][d_ff:].reshape(nch, 1, fc)
        wd = w_down[i].astype(BF16).reshape(nch, fc, d)

        qx, kx, vtx, ux, svx, xpx = _proj_call(x, mx[0], mx[1], g1n, w_in_b, e_q, e_k, gain,
                                               cos_x, sin_x, proj_x)
        if last:
            kc, vtc = _proj_call(ctx, mc[0], mc[1], g1n, w_in_b, e_q, e_k, gain,
                                 cos_c, sin_c, tile_c, kv_only=True)
        else:
            qc, kc, vtc, uc, svc, xpc = _proj_call(ctx, mc[0], mc[1], g1n, w_in_b, e_q, e_k, gain,
                                                   cos_c, sin_c, tile_c)
        score_bound = (1.01 * HEAD_DIM * jnp.max(jnp.abs(gain[0, :N_HEADS * HEAD_DIM]))
                       * jnp.max(jnp.abs(k_gain[i])))
        att_x = _attn_call(qx, [kx, kc], [vtx, vtc], big_x, score_bound)
        x = _mix_call(att_x, ux, svx, xpx, x, mx[2], ws_b, bias, wp_bd, pscale, icnt_x,
                      w_out_b, big_x)
        x = _ffn_call(x, mx[3], mx[4], mx[5], g2n, wg, wv, cwg, cwv, cbg, cbv, wd, fg,
                      big_x, last)
        if not last:
            att_c = _attn_call(qc, [kc], [vtc], tile_c)
            ctx = _mix_call(att_c, uc, svc, xpc, ctx, mc[2], ws_b, bias, wp_bd, pscale, icnt_c,
                            w_out_b, tile_c)
            ctx = _ffn_call(ctx, mc[3], mc[4], mc[5], g2n, wg, wv, cwg, cwv, cbg, cbv, wd, fg,
                            tile_c, False)
    return x
```

```python
import functools
import math

import numpy as np
import jax
import jax.numpy as jnp
from jax import lax
from jax.experimental import pallas as pl
from jax.experimental.pallas import tpu as pltpu

F32 = jnp.float32
BF16 = jnp.bfloat16

EPS = 1e-6
GRID_W = 64
HEAD_DIM = 64
N_HEADS = 8
N_KV_HEADS = 2
ROPE_BASE = 10000.0
CHUNK = 128
N_SGU_GROUPS = 4
POOL_WINDOWS = (2, 4, 8, 16)
POOL_GROUP_DIM = 64
LANES = 128
SUBLANES = 8
HALO = SUBLANES
VMEM_LIMIT = 56 * 1024 * 1024
FFN_Z_SLOTS = 3
ATTN_SLOTS = 2
ATTN_SUB = 256
ATTN_KEY_CHUNK = 1024
ATTN_MAX_SHIFTED_RANGE = 100.0
PROJ_SUB = 256


def _cparams(n_axes):
    return pltpu.CompilerParams(
        dimension_semantics=("arbitrary",) * n_axes, vmem_limit_bytes=VMEM_LIMIT)


def _const_spec(shape):
    nd = len(shape)
    return pl.BlockSpec(shape, lambda *_: (0,) * nd, pipeline_mode=pl.Buffered(1))


def _silu(x):
    return x * (1.0 / (1.0 + jnp.exp(-x)))


def _rms_mod(x, g, shift, scale):
    ms = jnp.mean(x * x, axis=-1, keepdims=True)
    y = x * lax.rsqrt(ms + EPS) * g
    return y * (1.0 + scale) + shift


def _mod_kernel(c_ref, w_ref, b_ref, o_ref):
    s = _silu(c_ref[...]).astype(BF16)
    o_ref[0, 0] = jnp.dot(s, w_ref[0].astype(BF16), preferred_element_type=F32) + b_ref[0]


def _mod_call(cvec, w_mod, b_mod):
    depth, d, n = w_mod.shape
    rows = cvec.shape[0]
    return pl.pallas_call(
        _mod_kernel,
        grid=(depth, n // d),
        in_specs=[
            pl.BlockSpec((rows, d), lambda i, j: (0, 0)),
            pl.BlockSpec((1, d, d), lambda i, j: (i, 0, j)),
            pl.BlockSpec((1, 1, d), lambda i, j: (i, 0, j)),
        ],
        out_specs=pl.BlockSpec((1, 1, rows, d), lambda i, j: (i, j, 0, 0)),
        out_shape=jax.ShapeDtypeStruct((depth, n // d, rows, d), F32),
        compiler_params=_cparams(2),
        name="mod",
    )(cvec, w_mod, b_mod.reshape(depth, 1, n))


def _proj_kernel(x_ref, sh_ref, sc_ref, g_ref, w_ref, eq_ref, ek_ref, gain_ref, cos_ref, sin_ref,
                 *out_refs, kv_only):
    if kv_only:
        k_ref, vt_ref = out_refs
    else:
        q_ref, k_ref, vt_ref, u_ref, sv_ref, xp_ref = out_refs
    t = x_ref.shape[1]
    sub = min(t, PROJ_SUB)
    nq = N_HEADS * HEAD_DIM
    nqk = (N_HEADS + N_KV_HEADS) * HEAD_DIM
    nv = N_KV_HEADS * HEAD_DIM
    col0 = nq if kv_only else 0
    lane = lax.broadcasted_iota(jnp.int32, (sub, LANES), 1)
    first = (lane % 32) < 16
    lo = lane < HEAD_DIM

    def project(r):
        rows = slice(sub * r, sub * (r + 1))
        h = _rms_mod(x_ref[0, rows], g_ref[...], sh_ref[0], sc_ref[0]).astype(BF16)
        w = w_ref[:, col0:nqk + nv] if kv_only else w_ref[...]
        return jnp.dot(h, w, preferred_element_type=F32)

    def finish(r, p):
        rows = slice(sub * r, sub * (r + 1))
        qk = p[:, :nqk - col0]
        sq = (qk * qk).astype(BF16)
        msk = jnp.dot(sq[:, nq - col0:], ek_ref[...], preferred_element_type=F32)
        if kv_only:
            msh = msk
        else:
            msh = jnp.concatenate(
                [jnp.dot(sq[:, :nq], eq_ref[...], preferred_element_type=F32), msk], axis=1)
        qkn = qk * lax.rsqrt(msh + EPS) * gain_ref[:, col0:]
        c = cos_ref[rows]
        s = sin_ref[rows]
        for j in range(col0 // LANES, nqk // LANES):
            blk = qkn[:, LANES * j - col0:LANES * (j + 1) - col0]
            sw = jnp.where(first, pltpu.roll(blk, LANES - 16, axis=1), pltpu.roll(blk, 16, axis=1))
            rot = blk * c + sw * s
            if j < N_HEADS // 2:
                kvh = (2 * j) // (N_HEADS // N_KV_HEADS)
                rr = pltpu.roll(rot, HEAD_DIM, axis=1)
                keep = lo if kvh == 0 else jnp.logical_not(lo)
                even = rot if kvh == 0 else rr
                odd = rr if kvh == 0 else rot
                q_ref[0, rows, LANES * (2 * j):LANES * (2 * j + 1)] = (
                    jnp.where(keep, even, 0.0).astype(BF16))
                q_ref[0, rows, LANES * (2 * j + 1):LANES * (2 * j + 2)] = (
                    jnp.where(keep, odd, 0.0).astype(BF16))
            else:
                k_ref[0, rows] = rot.astype(BF16)
        vt_ref[0, :, rows] = p[:, nqk - col0:nqk - col0 + nv].T.astype(BF16)
        if not kv_only:
            u_ref[0, rows] = p[:, 768:1024]
            sv_ref[0, rows] = p[:, 1024:1280].astype(BF16)
            xp_ref[0, rows] = p[:, 1280:1536]

    nsub = t // sub
    p_next = project(0)
    for r in range(nsub):
        p = p_next
        if r + 1 < nsub:
            p_next = project(r + 1)
        finish(r, p)


def _cond_index(cond_row):
    return (lambda b_: b_) if cond_row is None else (lambda b_: cond_row)


def _proj_call(x, sh, sc, g, w_in, e_q, e_k, gain, cos_t, sin_t, tile, cond_row=None,
               kv_only=False):
    b, l, d = x.shape
    nt = l // tile
    n_in = w_in.shape[1]
    crow = _cond_index(cond_row)
    row = lambda t_, b_: (b_, t_, 0)
    vec = lambda t_, b_: (crow(b_), 0, 0)
    tab = lambda t_, b_: (t_, 0)
    out_specs = [
        pl.BlockSpec((1, tile, N_HEADS * LANES), row),
        pl.BlockSpec((1, tile, LANES), row),
        pl.BlockSpec((1, LANES, tile), lambda t_, b_: (b_, 0, t_)),
        pl.BlockSpec((1, tile, 256), row),
        pl.BlockSpec((1, tile, 256), row),
        pl.BlockSpec((1, tile, 256), row),
    ]
    out_shape = [
        jax.ShapeDtypeStruct((b, l, N_HEADS * LANES), BF16),
        jax.ShapeDtypeStruct((b, l, LANES), BF16),
        jax.ShapeDtypeStruct((b, LANES, l), BF16),
        jax.ShapeDtypeStruct((b, l, 256), F32),
        jax.ShapeDtypeStruct((b, l, 256), BF16),
        jax.ShapeDtypeStruct((b, l, 256), F32),
    ]
    if kv_only:
        out_specs, out_shape = out_specs[1:3], out_shape[1:3]
    return pl.pallas_call(
        functools.partial(_proj_kernel, kv_only=kv_only),
        grid=(nt, b),
        in_specs=[
            pl.BlockSpec((1, tile, d), row),
            pl.BlockSpec((1, 1, d), vec),
            pl.BlockSpec((1, 1, d), vec),
            _const_spec((1, d)),
            _const_spec((d, n_in)),
            _const_spec(e_q.shape),
            _const_spec(e_k.shape),
            _const_spec(gain.shape),
            pl.BlockSpec((tile, LANES), tab),
            pl.BlockSpec((tile, LANES), tab),
        ],
        out_specs=out_specs,
        out_shape=out_shape,
        compiler_params=_cparams(2),
        name="proj_kv" if kv_only else "proj",
    )(x, sh, sc, g, w_in, e_q, e_k, gain, cos_t, sin_t)


def _attn_kernel(*refs, n_kv):
    q_ref = refs[0]
    k_refs = refs[1:1 + n_kv]
    vt_refs = refs[1 + n_kv:1 + 2 * n_kv]
    o_ref, s_ref, p_ref = refs[1 + 2 * n_kv:]
    group = N_HEADS // N_KV_HEADS
    nslot = s_ref.shape[0]
    sub = s_ref.shape[2]
    nunit = (q_ref.shape[1] // sub) * N_HEADS

    chunks = []
    base = 0
    for j, kr in enumerate(k_refs):
        n = kr.shape[1]
        step = min(n, ATTN_KEY_CHUNK)
        chunks += [(j, r0, step, base + r0) for r0 in range(0, n, step)]
        base += n

    def q_unit(u):
        r, i = divmod(u, N_HEADS)
        return q_ref[0, sub * r:sub * (r + 1), LANES * i:LANES * (i + 1)]

    pending = []

    def emit(u, ot, l):
        r, i = divmod(u, N_HEADS)
        kvh = i // group
        pending.append(ot[HEAD_DIM * kvh:HEAD_DIM * (kvh + 1)] * (1.0 / l))
        if i % 2 == 1:
            pair = jnp.concatenate(pending, axis=0)
            pending.clear()
            o_ref[0, sub * r:sub * (r + 1), LANES * (i // 2):LANES * (i // 2 + 1)] = (
                pair.T.astype(BF16))

    m = None
    l_prev = None
    for u in range(-1, nunit + 1):
        do_scores = u + 1 < nunit
        do_softmax = 0 <= u < nunit
        do_pv = 1 <= u
        q_next = q_unit(u + 1) if do_scores else None
        mparts, lparts, ot = [], [], None
        for j, r0, n, s0 in chunks:
            if do_scores:
                st = lax.dot_general(k_refs[j][0, r0:r0 + n], q_next, (((1,), (1,)), ((), ())),
                                     preferred_element_type=F32)
                s_ref[(u + 1) % nslot, s0:s0 + n] = st
                mparts.append(jnp.max(st, axis=0, keepdims=True))
            if do_softmax:
                p = jnp.exp2(s_ref[u % nslot, s0:s0 + n] - m)
                lparts.append(jnp.sum(p, axis=0, keepdims=True))
                p_ref[u % nslot, s0:s0 + n] = p.astype(BF16)
            if do_pv:
                d = jnp.dot(vt_refs[j][0, :, r0:r0 + n], p_ref[(u - 1) % nslot, s0:s0 + n],
                            preferred_element_type=F32)
                ot = d if ot is None else ot + d
        if do_pv:
            emit(u - 1, ot, l_prev)
        if do_softmax:
            l_prev = functools.reduce(jnp.add, lparts)
        if do_scores:
            m = functools.reduce(jnp.maximum, mparts)


def _attn_bounded_kernel(*refs, n_kv):
    q_ref = refs[0]
    k_refs = refs[1:1 + n_kv]
    vt_refs = refs[1 + n_kv:1 + 2 * n_kv]
    shift_ref, o_ref, p_ref = refs[1 + 2 * n_kv:]
    group = N_HEADS // N_KV_HEADS
    nslot = p_ref.shape[0]
    sub = p_ref.shape[2]
    nunit = (q_ref.shape[1] // sub) * N_HEADS
    shift = shift_ref[0]

    chunks = []
    base = 0
    for j, kr in enumerate(k_refs):
        n = kr.shape[1]
        step = min(n, ATTN_KEY_CHUNK)
        chunks += [(j, r0, step, base + r0) for r0 in range(0, n, step)]
        base += n

    def q_unit(u):
        r, i = divmod(u, N_HEADS)
        return q_ref[0, sub * r:sub * (r + 1), LANES * i:LANES * (i + 1)]

    def v_ext(kvh, j, r0, n):
        vt = vt_refs[j][0, HEAD_DIM * kvh:HEAD_DIM * (kvh + 1), r0:r0 + n]
        return jnp.concatenate([vt, jnp.ones((2 * SUBLANES, n), BF16)], axis=0)

    pending = []

    def emit(u, ot):
        r, i = divmod(u, N_HEADS)
        pending.append(ot[:HEAD_DIM] * (1.0 / ot[HEAD_DIM:HEAD_DIM + 1]))
        if i % 2 == 1:
            pair = jnp.concatenate(pending, axis=0)
            pending.clear()
            o_ref[0, sub * r:sub * (r + 1), LANES * (i // 2):LANES * (i // 2 + 1)] = (
                pair.T.astype(BF16))

    for u in range(nunit + 1):
        q_u = q_unit(u) if u < nunit else None
        kvh_prev = ((u - 1) % N_HEADS) // group
        ot = None
        for j, r0, n, s0 in chunks:
            if u < nunit:
                st = lax.dot_general(k_refs[j][0, r0:r0 + n], q_u, (((1,), (1,)), ((), ())),
                                     preferred_element_type=F32)
                p_ref[u % nslot, s0:s0 + n] = jnp.exp2(st - shift).astype(BF16)
            if u >= 1:
                d = jnp.dot(v_ext(kvh_prev, j, r0, n), p_ref[(u - 1) % nslot, s0:s0 + n],
                            preferred_element_type=F32)
                ot = d if ot is None else ot + d
        if u >= 1:
            emit(u - 1, ot)


def _attn_call(q, ks, vts, tq, score_bound=None):
    b, s, _ = q.shape
    n_kv = len(ks)
    nk = sum(k.shape[1] for k in ks)
    sub = min(tq, ATTN_SUB)
    kv_map = lambda b_, t_: (b_, 0, 0)
    in_specs = ([pl.BlockSpec((1, tq, N_HEADS * LANES), lambda b_, t_: (b_, t_, 0))]
                + [pl.BlockSpec((1, k.shape[1], LANES), kv_map) for k in ks]
                + [pl.BlockSpec((1, LANES, vt.shape[2]), kv_map) for vt in vts])
    common = dict(
        grid=(b, s // tq),
        out_specs=pl.BlockSpec((1, tq, N_HEADS * HEAD_DIM), lambda b_, t_: (b_, t_, 0)),
        out_shape=jax.ShapeDtypeStruct((b, s, N_HEADS * HEAD_DIM), BF16),
        compiler_params=_cparams(2),
    )

    def exact(_):
        return pl.pallas_call(
            functools.partial(_attn_kernel, n_kv=n_kv),
            in_specs=in_specs,
            scratch_shapes=[pltpu.VMEM((ATTN_SLOTS, nk, sub), F32),
                            pltpu.VMEM((ATTN_SLOTS, nk, sub), BF16)],
            name="attn", **common,
        )(q, *ks, *vts)

    if score_bound is None:
        return exact(None)

    def bounded(shift):
        return pl.pallas_call(
            functools.partial(_attn_bounded_kernel, n_kv=n_kv),
            in_specs=in_specs + [pl.BlockSpec(memory_space=pltpu.SMEM)],
            scratch_shapes=[pltpu.VMEM((ATTN_SLOTS, nk, sub), BF16)],
            name="attn_bounded", **common,
        )(q, *ks, *vts, shift.reshape(1))

    return lax.cond(2.0 * score_bound <= ATTN_MAX_SHIFTED_RANGE, bounded, exact, score_bound)


def _mix_kernel(att_ref, u_ref, sv_ref, xp_ref, xpp_ref, xpn_ref, x_ref, g1_ref,
                ws_ref, bias_ref, wp_ref, ps_ref, icnt_ref, wo_ref, o_ref):
    t = x_ref.shape[1]
    ti = pl.program_id(1)
    nt = pl.num_programs(1)
    lane = lax.broadcasted_iota(jnp.int32, (CHUNK, LANES), 1)
    lo = lane < 64

    sv = sv_ref[0]
    rows = []
    for n in range(t // CHUNK):
        cols = []
        for jp in range(N_SGU_GROUPS // 2):
            blk = sv[CHUNK * n:CHUNK * (n + 1), LANES * jp:LANES * (jp + 1)]
            r0 = jnp.dot(ws_ref[2 * jp], blk, preferred_element_type=F32)
            r1 = jnp.dot(ws_ref[2 * jp + 1], blk, preferred_element_type=F32)
            cols.append(jnp.where(lo, r0, r1))
        rows.append(jnp.concatenate(cols, axis=1) + bias_ref[...])
    sgu = u_ref[0] * jnp.concatenate(rows, axis=0)

    xp = xp_ref[0]
    prev = jnp.where(ti > 0, xpp_ref[0], 0.0)
    nxt = jnp.where(ti < nt - 1, xpn_ref[0], 0.0)
    xe = jnp.concatenate([prev, xp, nxt], axis=0)
    lane_t = lax.broadcasted_iota(jnp.int32, (t, LANES), 1)
    lo_t = lane_t < 64

    def shifted(col, k_):
        return xe[HALO + k_:HALO + k_ + t, LANES * col:LANES * (col + 1)]

    def window(col, w, inner=None, inner_w=0):
        left = w // 2
        right = w - 1 - left
        ileft = inner_w // 2
        iright = inner_w - 1 - ileft
        acc = inner
        for k_ in range(-left, right + 1):
            if inner is not None and -ileft <= k_ <= iright:
                continue
            term = shifted(col, k_)
            acc = term if acc is None else acc + term
        return acc

    w2 = window(0, 2)
    w4 = window(0, 4, w2, 2)
    w8 = window(1, 8)
    w16 = window(1, 16, w8, 8)
    tot = jnp.concatenate([jnp.where(lo_t, w2, w4), jnp.where(lo_t, w8, w16)], axis=1)
    y = tot * icnt_ref[...] - xp
    pool = jnp.dot(y.astype(BF16), wp_ref[...], preferred_element_type=F32) * ps_ref[...]

    na = att_ref.shape[2]
    ns = sgu.shape[1]
    mix = jnp.dot(att_ref[0], wo_ref[0:na, :], preferred_element_type=F32)
    mix = mix + jnp.dot(sgu.astype(BF16), wo_ref[na:na + ns, :], preferred_element_type=F32)
    mix = mix + jnp.dot(pool.astype(BF16), wo_ref[na + ns:, :], preferred_element_type=F32)
    o_ref[0] = x_ref[0] + g1_ref[0] * mix


def _mix_call(att, u, sv, xp, x, g1, ws, bias, wp_bd, pscale, icnt, w_out, tile, cond_row=None):
    b, l, d = x.shape
    nt = l // tile
    hb = tile // HALO
    nhb = l // HALO
    crow = _cond_index(cond_row)
    row = lambda b_, t_: (b_, t_, 0)
    vec = lambda b_, t_: (crow(b_), 0, 0)
    prev = lambda b_, t_: (b_, jnp.maximum(t_ * hb - 1, 0), 0)
    nxt = lambda b_, t_: (b_, jnp.minimum((t_ + 1) * hb, nhb - 1), 0)
    return pl.pallas_call(
        _mix_kernel,
        grid=(b, nt),
        in_specs=[
            pl.BlockSpec((1, tile, att.shape[2]), row),
            pl.BlockSpec((1, tile, 256), row),
            pl.BlockSpec((1, tile, 256), row),
            pl.BlockSpec((1, tile, 256), row),
            pl.BlockSpec((1, HALO, 256), prev),
            pl.BlockSpec((1, HALO, 256), nxt),
            pl.BlockSpec((1, tile, d), row),
            pl.BlockSpec((1, 1, d), vec),
            _const_spec(ws.shape),
            _const_spec(bias.shape),
            _const_spec(wp_bd.shape),
            _const_spec(pscale.shape),
            pl.BlockSpec((tile, 256), lambda b_, t_: (t_, 0)),
            _const_spec(w_out.shape),
        ],
        out_specs=pl.BlockSpec((1, tile, d), row),
        out_shape=jax.ShapeDtypeStruct((b, l, d), F32),
        compiler_params=_cparams(2),
        name="mix",
    )(att, u, sv, xp, xp, xp, x, g1, ws, bias, wp_bd, pscale, icnt, w_out)


def _ffn_kernel(x_ref, xp_ref, xn_ref, sh_ref, sc_ref, g2_ref, ng_ref, wg_ref, wv_ref,
                cwg_ref, cwv_ref, cbg_ref, cbv_ref, wd_ref, fg_ref, o_ref, z_ref, *, final):
    t = x_ref.shape[1]
    ti = pl.program_id(1)
    nt = pl.num_programs(1)
    x = x_ref[0]
    ng = ng_ref[...]
    sh = sh_ref[0]
    sc = sc_ref[0]
    hprev = jnp.where(ti > 0, _rms_mod(xp_ref[0], ng, sh, sc), 0.0)
    hnext = jnp.where(ti < nt - 1, _rms_mod(xn_ref[0], ng, sh, sc), 0.0)
    h = jnp.concatenate([hprev, _rms_mod(x, ng, sh, sc), hnext], axis=0).astype(BF16)
    nch = wg_ref.shape[0]

    nl = wg_ref.shape[2] // LANES
    nslot = z_ref.shape[0]

    def up(c):
        slot = c % nslot
        zg = jnp.dot(h, wg_ref[c], preferred_element_type=F32)
        zv = jnp.dot(h, wv_ref[c], preferred_element_type=F32)
        for j in range(nl):
            z_ref[slot, j] = zg[:, LANES * j:LANES * (j + 1)]
            z_ref[slot, nl + j] = zv[:, LANES * j:LANES * (j + 1)]

    def conv(slot, j, cw, cb):
        return (z_ref[slot, j, HALO - 1:HALO - 1 + t] * cw[0:1]
                + z_ref[slot, j, HALO:HALO + t] * cw[1:2]
                + z_ref[slot, j, HALO + 1:HALO + 1 + t] * cw[2:3] + cb)

    acc = jnp.zeros((t, x.shape[1]), F32)
    for c in range(min(nslot - 1, nch)):
        up(c)
    for c in range(nch):
        if c + nslot - 1 < nch:
            up(c + nslot - 1)
        slot = c % nslot
        parts = []
        for j in range(nl):
            ls = slice(LANES * j, LANES * (j + 1))
            cg = conv(slot, j, cwg_ref[c][:, ls], cbg_ref[c][:, ls])
            cv = conv(slot, nl + j, cwv_ref[c][:, ls], cbv_ref[c][:, ls])
            parts.append((_silu(cg) * cv).astype(BF16))
        a = jnp.concatenate(parts, axis=1)
        acc = acc + jnp.dot(a, wd_ref[c], preferred_element_type=F32)
    y = x + g2_ref[0] * acc
    if final:
        ms = jnp.mean(y * y, axis=-1, keepdims=True)
        y = y * lax.rsqrt(ms + EPS) * fg_ref[...]
    o_ref[0] = y


def _ffn_call(x, sh, sc, g2, ng, wg, wv, cwg, cwv, cbg, cbv, wd, fg, tile, final, cond_row=None):
    b, l, d = x.shape
    nt = l // tile
    hb = tile // HALO
    nhb = l // HALO
    crow = _cond_index(cond_row)
    row = lambda b_, t_: (b_, t_, 0)
    vec = lambda b_, t_: (crow(b_), 0, 0)
    prev = lambda b_, t_: (b_, jnp.maximum(t_ * hb - 1, 0), 0)
    nxt = lambda b_, t_: (b_, jnp.minimum((t_ + 1) * hb, nhb - 1), 0)
    return pl.pallas_call(
        functools.partial(_ffn_kernel, final=final),
        grid=(b, nt),
        in_specs=[
            pl.BlockSpec((1, tile, d), row),
            pl.BlockSpec((1, HALO, d), prev),
            pl.BlockSpec((1, HALO, d), nxt),
            pl.BlockSpec((1, 1, d), vec),
            pl.BlockSpec((1, 1, d), vec),
            pl.BlockSpec((1, 1, d), vec),
            _const_spec(ng.shape),
            _const_spec(wg.shape),
            _const_spec(wv.shape),
            _const_spec(cwg.shape),
            _const_spec(cwv.shape),
            _const_spec(cbg.shape),
            _const_spec(cbv.shape),
            _const_spec(wd.shape),
            _const_spec(fg.shape),
        ],
        out_specs=pl.BlockSpec((1, tile, d), row),
        out_shape=jax.ShapeDtypeStruct((b, l, d), F32),
        scratch_shapes=[pltpu.VMEM((FFN_Z_SLOTS, 2 * wg.shape[2] // LANES, tile + 2 * HALO, LANES), F32)],
        compiler_params=_cparams(2),
        name="ffn_final" if final else "ffn",
    )(x, x, x, sh, sc, g2, ng, wg, wv, cwg, cwv, cbg, cbv, wd, fg)


def _rope_tables(n):
    pos = jnp.arange(n)
    row = (pos // GRID_W).astype(F32)
    col = (pos % GRID_W).astype(F32)
    axis_dim = HEAD_DIM // 2
    inv = ROPE_BASE ** (-jnp.arange(0, axis_dim, 2, dtype=F32) / axis_dim)
    ang_r = row[:, None] * inv[None, :]
    ang_c = col[:, None] * inv[None, :]
    cr, sr, cc, sn = jnp.cos(ang_r), jnp.sin(ang_r), jnp.cos(ang_c), jnp.sin(ang_c)
    cos_h = jnp.concatenate([cr, cr, cc, cc], axis=1)
    sin_h = jnp.concatenate([-sr, sr, -sn, sn], axis=1)
    return jnp.tile(cos_h, (1, 2)), jnp.tile(sin_h, (1, 2))


def _pool_inv_count(l):
    t = np.arange(l)
    cols = []
    for w in POOL_WINDOWS:
        left = w // 2
        right = w - 1 - left
        cnt = (np.minimum(t + right, l - 1) - np.maximum(t - left, 0) + 1).astype(np.float32)
        cols.append(np.repeat((np.float32(1.0) / cnt)[:, None], POOL_GROUP_DIM, axis=1))
    return jnp.asarray(np.concatenate(cols, axis=1))


def _block_diag(w):
    g, a, b = w.shape
    out = jnp.zeros((g * a, g * b), w.dtype)
    for i in range(g):
        out = out.at[i * a:(i + 1) * a, i * b:(i + 1) * b].set(w[i])
    return out


def _head_mean_matrix(n_heads):
    n = n_heads * HEAD_DIM
    idx = np.arange(n) // HEAD_DIM
    return jnp.asarray((idx[:, None] == idx[None, :]).astype(np.float32) / HEAD_DIM, dtype=BF16)


def kernel(x, c, ctx, c_ctx, w_mod, b_mod, norm1_g, w_in, q_gain, k_gain, w_s, b_s, w_pool,
           pool_scale, w_out, norm2_g, w_up, conv_w, conv_b, w_down, final_g):
    b, s, d = x.shape
    cl = ctx.shape[1]
    depth = w_mod.shape[0]
    d_ff = w_down.shape[1]
    fc = 256
    nch = d_ff // fc

    rows = ((b + 1 + SUBLANES - 1) // SUBLANES) * SUBLANES
    cvec = jnp.zeros((rows, d), F32).at[:b].set(c).at[b].set(c_ctx)
    mod = _mod_call(cvec, w_mod, b_mod)

    cos_x, sin_x = _rope_tables(s)
    cos_c = jnp.ones((cl, LANES), F32)
    sin_c = jnp.zeros((cl, LANES), F32)
    e_q = _head_mean_matrix(N_HEADS)
    e_k = _head_mean_matrix(N_KV_HEADS)
    icnt_x = _pool_inv_count(s)
    icnt_c = _pool_inv_count(cl)
    fg = final_g.reshape(1, d)

    tile_c = min(256, cl)
    big_x = min(512, s)
    wide_x = min(1024, s)

    for i in range(depth):
        last = i == depth - 1
        m6 = [mod[i, j].reshape(rows, 1, d) for j in range(6)]
        g1n = norm1_g[i].reshape(1, d)
        g2n = norm2_g[i].reshape(1, d)
        w_in_b = w_in[i].astype(BF16)
        w_out_b = w_out[i].astype(BF16)
        gain = jnp.concatenate([jnp.tile(q_gain[i] * (HEAD_DIM ** -0.5 * math.log2(math.e)), N_HEADS),
                                jnp.tile(k_gain[i], N_KV_HEADS)]).reshape(1, -1)
        ws_b = w_s[i].astype(BF16)
        bias = jnp.repeat(b_s[i].T, d // 4 // N_SGU_GROUPS, axis=1)
        wp_bd = _block_diag(w_pool[i]).astype(BF16)
        pscale = pool_scale[i].reshape(1, -1)
        wup = w_up[i].astype(BF16)
        wg = wup[:, :d_ff].reshape(d, nch, fc).transpose(1, 0, 2)
        wv = wup[:, d_ff:].reshape(d, nch, fc).transpose(1, 0, 2)
        cwg = conv_w[i][:, :d_ff].reshape(3, nch, fc).transpose(1, 0, 2)
        cwv = conv_w[i][:, d_ff:].reshape(3, nch, fc).transpose(1, 0, 2)
        cbg = conv_b[i][:d_ff].reshape(nch, 1, fc)
        cbv = conv_b[i][d_ff:].reshape(nch, 1, fc)
        wd = w_down[i].astype(BF16).reshape(nch, fc, d)

        qx, kx, vtx, ux, svx, xpx = _proj_call(x, m6[0], m6[1], g1n, w_in_b, e_q, e_k, gain,
                                               cos_x, sin_x, wide_x)
        if last:
            kc, vtc = _proj_call(ctx, m6[0], m6[1], g1n, w_in_b, e_q, e_k, gain,
                                 cos_c, sin_c, tile_c, cond_row=b, kv_only=True)
        else:
            qc, kc, vtc, uc, svc, xpc = _proj_call(ctx, m6[0], m6[1], g1n, w_in_b, e_q, e_k, gain,
                                                   cos_c, sin_c, tile_c, cond_row=b)
        score_bound = (1.01 * HEAD_DIM * jnp.max(jnp.abs(gain[0, :N_HEADS * HEAD_DIM]))
                       * jnp.max(jnp.abs(k_gain[i])))
        att_x = _attn_call(qx, [kx, kc], [vtx, vtc], wide_x, score_bound)
        x = _mix_call(att_x, ux, svx, xpx, x, m6[2], ws_b, bias, wp_bd, pscale, icnt_x,
                      w_out_b, wide_x)
        x = _ffn_call(x, m6[3], m6[4], m6[5], g2n, wg, wv, cwg, cwv, cbg, cbv, wd, fg,
                      big_x, last)
        if not last:
            att_c = _attn_call(qc, [kc], [vtc], tile_c)
            ctx = _mix_call(att_c, uc, svc, xpc, ctx, m6[2], ws_b, bias, wp_bd, pscale, icnt_c,
                            w_out_b, tile_c, cond_row=b)
            ctx = _ffn_call(ctx, m6[3], m6[4], m6[5], g2n, wg, wv, cwg, cwv, cbg, cbv, wd, fg,
                            tile_c, False, cond_row=b)
    return x
```

```python
import functools
import math

import numpy as np
import jax
import jax.numpy as jnp
from jax import lax
from jax.experimental import pallas as pl
from jax.experimental.pallas import tpu as pltpu

F32 = jnp.float32
BF16 = jnp.bfloat16

EPS = 1e-6
GRID_W = 64
HEAD_DIM = 64
N_HEADS = 8
N_KV_HEADS = 2
ROPE_BASE = 10000.0
CHUNK = 128
N_SGU_GROUPS = 4
POOL_WINDOWS = (2, 4, 8, 16)
POOL_GROUP_DIM = 64
LANES = 128
SUBLANES = 8
HALO = SUBLANES
VMEM_LIMIT = 56 * 1024 * 1024
FFN_CHUNK = 256
FFN_Z_SLOTS = 3
ATTN_SLOTS = 2
ATTN_SUB = 256
ATTN_KEY_CHUNK = 1024
ATTN_MAX_SHIFTED_RANGE = 100.0
PROJ_SUB = 256


def _cparams(n_axes):
    return pltpu.CompilerParams(
        dimension_semantics=("arbitrary",) * n_axes, vmem_limit_bytes=VMEM_LIMIT)


def _const_spec(shape):
    nd = len(shape)
    return pl.BlockSpec(shape, lambda *_: (0,) * nd, pipeline_mode=pl.Buffered(1))


def _silu(x):
    return x * (1.0 / (1.0 + jnp.exp(-x)))


def _rms_mod(x, g, shift, scale):
    ms = jnp.mean(x * x, axis=-1, keepdims=True)
    y = x * lax.rsqrt(ms + EPS) * g
    return y * (1.0 + scale) + shift


def _mod_kernel(c_ref, w_ref, b_ref, o_ref):
    s = _silu(c_ref[...]).astype(BF16)
    o_ref[0, 0] = jnp.dot(s, w_ref[0].astype(BF16), preferred_element_type=F32) + b_ref[0]


def _mod_call(cvec, w_mod, b_mod):
    depth, d, n = w_mod.shape
    rows = cvec.shape[0]
    return pl.pallas_call(
        _mod_kernel,
        grid=(depth, n // d),
        in_specs=[
            pl.BlockSpec((rows, d), lambda i, j: (0, 0)),
            pl.BlockSpec((1, d, d), lambda i, j: (i, 0, j)),
            pl.BlockSpec((1, 1, d), lambda i, j: (i, 0, j)),
        ],
        out_specs=pl.BlockSpec((1, 1, rows, d), lambda i, j: (i, j, 0, 0)),
        out_shape=jax.ShapeDtypeStruct((depth, n // d, rows, d), F32),
        compiler_params=_cparams(2),
        name="mod",
    )(cvec, w_mod, b_mod.reshape(depth, 1, n))


def _proj_kernel(x_ref, sh_ref, sc_ref, g_ref, w_ref, eq_ref, ek_ref, gain_ref, cos_ref, sin_ref,
                 *out_refs, kv_only):
    if kv_only:
        k_ref, vt_ref = out_refs
    else:
        q_ref, k_ref, vt_ref, u_ref, sv_ref, xp_ref = out_refs
    t = x_ref.shape[1]
    sub = min(t, PROJ_SUB)
    nq = N_HEADS * HEAD_DIM
    nqk = (N_HEADS + N_KV_HEADS) * HEAD_DIM
    nv = N_KV_HEADS * HEAD_DIM
    col0 = nq if kv_only else 0
    lane = lax.broadcasted_iota(jnp.int32, (sub, LANES), 1)
    first = (lane % 32) < 16
    lo = lane < HEAD_DIM

    def project(r):
        rows = slice(sub * r, sub * (r + 1))
        h = _rms_mod(x_ref[0, rows], g_ref[...], sh_ref[0], sc_ref[0]).astype(BF16)
        w = w_ref[:, col0:nqk + nv] if kv_only else w_ref[...]
        return jnp.dot(h, w, preferred_element_type=F32)

    def finish(r, p):
        rows = slice(sub * r, sub * (r + 1))
        qk = p[:, :nqk - col0]
        sq = (qk * qk).astype(BF16)
        msk = jnp.dot(sq[:, nq - col0:], ek_ref[...], preferred_element_type=F32)
        if kv_only:
            msh = msk
        else:
            msh = jnp.concatenate(
                [jnp.dot(sq[:, :nq], eq_ref[...], preferred_element_type=F32), msk], axis=1)
        qkn = qk * lax.rsqrt(msh + EPS) * gain_ref[:, col0:]
        c = cos_ref[rows]
        s = sin_ref[rows]
        for j in range(col0 // LANES, nqk // LANES):
            blk = qkn[:, LANES * j - col0:LANES * (j + 1) - col0]
            sw = jnp.where(first, pltpu.roll(blk, LANES - 16, axis=1), pltpu.roll(blk, 16, axis=1))
            rot = blk * c + sw * s
            if j < N_HEADS // 2:
                kvh = (2 * j) // (N_HEADS // N_KV_HEADS)
                rr = pltpu.roll(rot, HEAD_DIM, axis=1)
                keep = lo if kvh == 0 else jnp.logical_not(lo)
                even = rot if kvh == 0 else rr
                odd = rr if kvh == 0 else rot
                q_ref[0, rows, LANES * (2 * j):LANES * (2 * j + 1)] = (
                    jnp.where(keep, even, 0.0).astype(BF16))
                q_ref[0, rows, LANES * (2 * j + 1):LANES * (2 * j + 2)] = (
                    jnp.where(keep, odd, 0.0).astype(BF16))
            else:
                k_ref[0, rows] = rot.astype(BF16)
        vt_ref[0, :, rows] = p[:, nqk - col0:nqk - col0 + nv].T.astype(BF16)
        if not kv_only:
            u_ref[0, rows] = p[:, 768:1024]
            sv_ref[0, rows] = p[:, 1024:1280].astype(BF16)
            xp_ref[0, rows] = p[:, 1280:1536]

    nsub = t // sub
    p_next = project(0)
    for r in range(nsub):
        p = p_next
        if r + 1 < nsub:
            p_next = project(r + 1)
        finish(r, p)


def _cond_index(cond_row):
    return (lambda b_: b_) if cond_row is None else (lambda b_: cond_row)


def _proj_call(x, sh, sc, g, w_in, e_q, e_k, gain, cos_t, sin_t, tile, cond_row=None,
               kv_only=False):
    b, l, d = x.shape
    nt = l // tile
    n_in = w_in.shape[1]
    crow = _cond_index(cond_row)
    row = lambda t_, b_: (b_, t_, 0)
    vec = lambda t_, b_: (crow(b_), 0, 0)
    tab = lambda t_, b_: (t_, 0)
    out_specs = [
        pl.BlockSpec((1, tile, N_HEADS * LANES), row),
        pl.BlockSpec((1, tile, LANES), row),
        pl.BlockSpec((1, LANES, tile), lambda t_, b_: (b_, 0, t_)),
        pl.BlockSpec((1, tile, 256), row),
        pl.BlockSpec((1, tile, 256), row),
        pl.BlockSpec((1, tile, 256), row),
    ]
    out_shape = [
        jax.ShapeDtypeStruct((b, l, N_HEADS * LANES), BF16),
        jax.ShapeDtypeStruct((b, l, LANES), BF16),
        jax.ShapeDtypeStruct((b, LANES, l), BF16),
        jax.ShapeDtypeStruct((b, l, 256), F32),
        jax.ShapeDtypeStruct((b, l, 256), BF16),
        jax.ShapeDtypeStruct((b, l, 256), F32),
    ]
    if kv_only:
        out_specs, out_shape = out_specs[1:3], out_shape[1:3]
    return pl.pallas_call(
        functools.partial(_proj_kernel, kv_only=kv_only),
        grid=(nt, b),
        in_specs=[
            pl.BlockSpec((1, tile, d), row),
            pl.BlockSpec((1, 1, d), vec),
            pl.BlockSpec((1, 1, d), vec),
            _const_spec((1, d)),
            _const_spec((d, n_in)),
            _const_spec(e_q.shape),
            _const_spec(e_k.shape),
            _const_spec(gain.shape),
            pl.BlockSpec((tile, LANES), tab),
            pl.BlockSpec((tile, LANES), tab),
        ],
        out_specs=out_specs,
        out_shape=out_shape,
        compiler_params=_cparams(2),
        name="proj_kv" if kv_only else "proj",
    )(x, sh, sc, g, w_in, e_q, e_k, gain, cos_t, sin_t)


def _attn_kernel(*refs, n_kv):
    q_ref = refs[0]
    k_refs = refs[1:1 + n_kv]
    vt_refs = refs[1 + n_kv:1 + 2 * n_kv]
    o_ref, s_ref, p_ref = refs[1 + 2 * n_kv:]
    group = N_HEADS // N_KV_HEADS
    nslot = s_ref.shape[0]
    sub = s_ref.shape[2]
    nunit = (q_ref.shape[1] // sub) * N_HEADS

    chunks = []
    base = 0
    for j, kr in enumerate(k_refs):
        n = kr.shape[1]
        step = min(n, ATTN_KEY_CHUNK)
        chunks += [(j, r0, step, base + r0) for r0 in range(0, n, step)]
        base += n

    def q_unit(u):
        r, i = divmod(u, N_HEADS)
        return q_ref[0, sub * r:sub * (r + 1), LANES * i:LANES * (i + 1)]

    pending = []

    def emit(u, ot, l):
        r, i = divmod(u, N_HEADS)
        kvh = i // group
        pending.append(ot[HEAD_DIM * kvh:HEAD_DIM * (kvh + 1)] * (1.0 / l))
        if i % 2 == 1:
            pair = jnp.concatenate(pending, axis=0)
            pending.clear()
            o_ref[0, sub * r:sub * (r + 1), LANES * (i // 2):LANES * (i // 2 + 1)] = (
                pair.T.astype(BF16))

    m = None
    l_prev = None
    for u in range(-1, nunit + 1):
        do_scores = u + 1 < nunit
        do_softmax = 0 <= u < nunit
        do_pv = 1 <= u
        q_next = q_unit(u + 1) if do_scores else None
        mparts, lparts, ot = [], [], None
        for j, r0, n, s0 in chunks:
            if do_scores:
                st = lax.dot_general(k_refs[j][0, r0:r0 + n], q_next, (((1,), (1,)), ((), ())),
                                     preferred_element_type=F32)
                s_ref[(u + 1) % nslot, s0:s0 + n] = st
                mparts.append(jnp.max(st, axis=0, keepdims=True))
            if do_softmax:
                p = jnp.exp2(s_ref[u % nslot, s0:s0 + n] - m)
                lparts.append(jnp.sum(p, axis=0, keepdims=True))
                p_ref[u % nslot, s0:s0 + n] = p.astype(BF16)
            if do_pv:
                d = jnp.dot(vt_refs[j][0, :, r0:r0 + n], p_ref[(u - 1) % nslot, s0:s0 + n],
                            preferred_element_type=F32)
                ot = d if ot is None else ot + d
        if do_pv:
            emit(u - 1, ot, l_prev)
        if do_softmax:
            l_prev = functools.reduce(jnp.add, lparts)
        if do_scores:
            m = functools.reduce(jnp.maximum, mparts)


def _attn_bounded_kernel(*refs, n_kv):
    q_ref = refs[0]
    k_refs = refs[1:1 + n_kv]
    vt_refs = refs[1 + n_kv:1 + 2 * n_kv]
    shift_ref, o_ref, p_ref = refs[1 + 2 * n_kv:]
    group = N_HEADS // N_KV_HEADS
    nslot = p_ref.shape[0]
    sub = p_ref.shape[2]
    nunit = (q_ref.shape[1] // sub) * N_HEADS
    shift = shift_ref[0]

    chunks = []
    base = 0
    for j, kr in enumerate(k_refs):
        n = kr.shape[1]
        step = min(n, ATTN_KEY_CHUNK)
        chunks += [(j, r0, step, base + r0) for r0 in range(0, n, step)]
        base += n

    def q_unit(u):
        r, i = divmod(u, N_HEADS)
        return q_ref[0, sub * r:sub * (r + 1), LANES * i:LANES * (i + 1)]

    def v_ext(kvh, j, r0, n):
        vt = vt_refs[j][0, HEAD_DIM * kvh:HEAD_DIM * (kvh + 1), r0:r0 + n]
        return jnp.concatenate([vt, jnp.ones((2 * SUBLANES, n), BF16)], axis=0)

    pending = []

    def emit(u, ot):
        r, i = divmod(u, N_HEADS)
        pending.append(ot[:HEAD_DIM] * (1.0 / ot[HEAD_DIM:HEAD_DIM + 1]))
        if i % 2 == 1:
            pair = jnp.concatenate(pending, axis=0)
            pending.clear()
            o_ref[0, sub * r:sub * (r + 1), LANES * (i // 2):LANES * (i // 2 + 1)] = (
                pair.T.astype(BF16))

    for u in range(nunit + 1):
        q_u = q_unit(u) if u < nunit else None
        kvh_prev = ((u - 1) % N_HEADS) // group
        ot = None
        for j, r0, n, s0 in chunks:
            if u < nunit:
                st = lax.dot_general(k_refs[j][0, r0:r0 + n], q_u, (((1,), (1,)), ((), ())),
                                     preferred_element_type=F32)
                p_ref[u % nslot, s0:s0 + n] = jnp.exp2(st - shift).astype(BF16)
            if u >= 1:
                d = jnp.dot(v_ext(kvh_prev, j, r0, n), p_ref[(u - 1) % nslot, s0:s0 + n],
                            preferred_element_type=F32)
                ot = d if ot is None else ot + d
        if u >= 1:
            emit(u - 1, ot)


def _attn_call(q, ks, vts, tq, score_bound=None):
    b, s, _ = q.shape
    n_kv = len(ks)
    nk = sum(k.shape[1] for k in ks)
    sub = min(tq, ATTN_SUB)
    kv_map = lambda b_, t_: (b_, 0, 0)
    in_specs = ([pl.BlockSpec((1, tq, N_HEADS * LANES), lambda b_, t_: (b_, t_, 0))]
                + [pl.BlockSpec((1, k.shape[1], LANES), kv_map) for k in ks]
                + [pl.BlockSpec((1, LANES, vt.shape[2]), kv_map) for vt in vts])
    common = dict(
        grid=(b, s // tq),
        out_specs=pl.BlockSpec((1, tq, N_HEADS * HEAD_DIM), lambda b_, t_: (b_, t_, 0)),
        out_shape=jax.ShapeDtypeStruct((b, s, N_HEADS * HEAD_DIM), BF16),
        compiler_params=_cparams(2),
    )

    def exact(_):
        return pl.pallas_call(
            functools.partial(_attn_kernel, n_kv=n_kv),
            in_specs=in_specs,
            scratch_shapes=[pltpu.VMEM((ATTN_SLOTS, nk, sub), F32),
                            pltpu.VMEM((ATTN_SLOTS, nk, sub), BF16)],
            name="attn", **common,
        )(q, *ks, *vts)

    if score_bound is None:
        return exact(None)

    def bounded(shift):
        return pl.pallas_call(
            functools.partial(_attn_bounded_kernel, n_kv=n_kv),
            in_specs=in_specs + [pl.BlockSpec(memory_space=pltpu.SMEM)],
            scratch_shapes=[pltpu.VMEM((ATTN_SLOTS, nk, sub), BF16)],
            name="attn_bounded", **common,
        )(q, *ks, *vts, shift.reshape(1))

    return lax.cond(2.0 * score_bound <= ATTN_MAX_SHIFTED_RANGE, bounded, exact, score_bound)


def _mix_kernel(att_ref, u_ref, sv_ref, xp_ref, xpp_ref, xpn_ref, x_ref, g1_ref,
                ws_ref, bias_ref, wp_ref, ps_ref, icnt_ref, wo_ref, o_ref):
    t = x_ref.shape[1]
    ti = pl.program_id(1)
    nt = pl.num_programs(1)
    lane = lax.broadcasted_iota(jnp.int32, (CHUNK, LANES), 1)
    lo = lane < 64

    sv = sv_ref[0]
    rows = []
    for n in range(t // CHUNK):
        cols = []
        for jp in range(N_SGU_GROUPS // 2):
            blk = sv[CHUNK * n:CHUNK * (n + 1), LANES * jp:LANES * (jp + 1)]
            r0 = jnp.dot(ws_ref[2 * jp], blk, preferred_element_type=F32)
            r1 = jnp.dot(ws_ref[2 * jp + 1], blk, preferred_element_type=F32)
            cols.append(jnp.where(lo, r0, r1))
        rows.append(jnp.concatenate(cols, axis=1) + bias_ref[...])
    sgu = u_ref[0] * jnp.concatenate(rows, axis=0)

    xp = xp_ref[0]
    prev = jnp.where(ti > 0, xpp_ref[0], 0.0)
    nxt = jnp.where(ti < nt - 1, xpn_ref[0], 0.0)
    xe = jnp.concatenate([prev, xp, nxt], axis=0)
    lane_t = lax.broadcasted_iota(jnp.int32, (t, LANES), 1)
    lo_t = lane_t < 64

    def shifted(col, k_):
        return xe[HALO + k_:HALO + k_ + t, LANES * col:LANES * (col + 1)]

    def window(col, w, inner=None, inner_w=0):
        left = w // 2
        right = w - 1 - left
        ileft = inner_w // 2
        iright = inner_w - 1 - ileft
        acc = inner
        for k_ in range(-left, right + 1):
            if inner is not None and -ileft <= k_ <= iright:
                continue
            term = shifted(col, k_)
            acc = term if acc is None else acc + term
        return acc

    w2 = window(0, 2)
    w4 = window(0, 4, w2, 2)
    w8 = window(1, 8)
    w16 = window(1, 16, w8, 8)
    tot = jnp.concatenate([jnp.where(lo_t, w2, w4), jnp.where(lo_t, w8, w16)], axis=1)
    y = tot * icnt_ref[...] - xp
    pool = jnp.dot(y.astype(BF16), wp_ref[...], preferred_element_type=F32) * ps_ref[...]

    na = att_ref.shape[2]
    ns = sgu.shape[1]
    mix = jnp.dot(att_ref[0], wo_ref[0:na, :], preferred_element_type=F32)
    mix = mix + jnp.dot(sgu.astype(BF16), wo_ref[na:na + ns, :], preferred_element_type=F32)
    mix = mix + jnp.dot(pool.astype(BF16), wo_ref[na + ns:, :], preferred_element_type=F32)
    o_ref[0] = x_ref[0] + g1_ref[0] * mix


def _mix_call(att, u, sv, xp, x, g1, ws, bias, wp_bd, pscale, icnt, w_out, tile, cond_row=None):
    b, l, d = x.shape
    nt = l // tile
    hb = tile // HALO
    nhb = l // HALO
    crow = _cond_index(cond_row)
    row = lambda b_, t_: (b_, t_, 0)
    vec = lambda b_, t_: (crow(b_), 0, 0)
    prev = lambda b_, t_: (b_, jnp.maximum(t_ * hb - 1, 0), 0)
    nxt = lambda b_, t_: (b_, jnp.minimum((t_ + 1) * hb, nhb - 1), 0)
    return pl.pallas_call(
        _mix_kernel,
        grid=(b, nt),
        in_specs=[
            pl.BlockSpec((1, tile, att.shape[2]), row),
            pl.BlockSpec((1, tile, 256), row),
            pl.BlockSpec((1, tile, 256), row),
            pl.BlockSpec((1, tile, 256), row),
            pl.BlockSpec((1, HALO, 256), prev),
            pl.BlockSpec((1, HALO, 256), nxt),
            pl.BlockSpec((1, tile, d), row),
            pl.BlockSpec((1, 1, d), vec),
            _const_spec(ws.shape),
            _const_spec(bias.shape),
            _const_spec(wp_bd.shape),
            _const_spec(pscale.shape),
            pl.BlockSpec((tile, 256), lambda b_, t_: (t_, 0)),
            _const_spec(w_out.shape),
        ],
        out_specs=pl.BlockSpec((1, tile, d), row),
        out_shape=jax.ShapeDtypeStruct((b, l, d), F32),
        compiler_params=_cparams(2),
        name="mix",
    )(att, u, sv, xp, xp, xp, x, g1, ws, bias, wp_bd, pscale, icnt, w_out)


def _ffn_kernel(x_ref, xp_ref, xn_ref, sh_ref, sc_ref, g2_ref, ng_ref, wup_ref, cw_ref, cb_ref,
                wd_ref, fg_ref, o_ref, z_ref, *, final):
    t = x_ref.shape[1]
    ti = pl.program_id(1)
    nt = pl.num_programs(1)
    x = x_ref[0]
    ng = ng_ref[...]
    sh = sh_ref[0]
    sc = sc_ref[0]
    hprev = jnp.where(ti > 0, _rms_mod(xp_ref[0], ng, sh, sc), 0.0)
    hnext = jnp.where(ti < nt - 1, _rms_mod(xn_ref[0], ng, sh, sc), 0.0)
    h = jnp.concatenate([hprev, _rms_mod(x, ng, sh, sc), hnext], axis=0).astype(BF16)
    nch, fc, _ = wd_ref.shape
    d_ff = nch * fc

    nl = fc // LANES
    nslot = z_ref.shape[0]

    def up(c):
        slot = c % nslot
        zg = jnp.dot(h, wup_ref[:, fc * c:fc * (c + 1)], preferred_element_type=F32)
        zv = jnp.dot(h, wup_ref[:, d_ff + fc * c:d_ff + fc * (c + 1)], preferred_element_type=F32)
        for j in range(nl):
            z_ref[slot, j] = zg[:, LANES * j:LANES * (j + 1)]
            z_ref[slot, nl + j] = zv[:, LANES * j:LANES * (j + 1)]

    def conv(slot, j, cw, cb):
        return (z_ref[slot, j, HALO - 1:HALO - 1 + t] * cw[0:1]
                + z_ref[slot, j, HALO:HALO + t] * cw[1:2]
                + z_ref[slot, j, HALO + 1:HALO + 1 + t] * cw[2:3] + cb)

    acc = jnp.zeros((t, x.shape[1]), F32)
    for c in range(min(nslot - 1, nch)):
        up(c)
    for c in range(nch):
        if c + nslot - 1 < nch:
            up(c + nslot - 1)
        slot = c % nslot
        parts = []
        for j in range(nl):
            lg = slice(fc * c + LANES * j, fc * c + LANES * (j + 1))
            lv = slice(d_ff + fc * c + LANES * j, d_ff + fc * c + LANES * (j + 1))
            cg = conv(slot, j, cw_ref[:, lg], cb_ref[:, lg])
            cv = conv(slot, nl + j, cw_ref[:, lv], cb_ref[:, lv])
            parts.append((_silu(cg) * cv).astype(BF16))
        a = jnp.concatenate(parts, axis=1)
        acc = acc + jnp.dot(a, wd_ref[c], preferred_element_type=F32)
    y = x + g2_ref[0] * acc
    if final:
        ms = jnp.mean(y * y, axis=-1, keepdims=True)
        y = y * lax.rsqrt(ms + EPS) * fg_ref[...]
    o_ref[0] = y


def _ffn_call(x, sh, sc, g2, ng, wup, cw, cb, wd, fg, tile, final, cond_row=None):
    b, l, d = x.shape
    nt = l // tile
    hb = tile // HALO
    nhb = l // HALO
    crow = _cond_index(cond_row)
    row = lambda b_, t_: (b_, t_, 0)
    vec = lambda b_, t_: (crow(b_), 0, 0)
    prev = lambda b_, t_: (b_, jnp.maximum(t_ * hb - 1, 0), 0)
    nxt = lambda b_, t_: (b_, jnp.minimum((t_ + 1) * hb, nhb - 1), 0)
    return pl.pallas_call(
        functools.partial(_ffn_kernel, final=final),
        grid=(b, nt),
        in_specs=[
            pl.BlockSpec((1, tile, d), row),
            pl.BlockSpec((1, HALO, d), prev),
            pl.BlockSpec((1, HALO, d), nxt),
            pl.BlockSpec((1, 1, d), vec),
            pl.BlockSpec((1, 1, d), vec),
            pl.BlockSpec((1, 1, d), vec),
            _const_spec(ng.shape),
            _const_spec(wup.shape),
            _const_spec(cw.shape),
            _const_spec(cb.shape),
            _const_spec(wd.shape),
            _const_spec(fg.shape),
        ],
        out_specs=pl.BlockSpec((1, tile, d), row),
        out_shape=jax.ShapeDtypeStruct((b, l, d), F32),
        scratch_shapes=[pltpu.VMEM(
            (FFN_Z_SLOTS, 2 * wd.shape[1] // LANES, tile + 2 * HALO, LANES), F32)],
        compiler_params=_cparams(2),
        name="ffn_final" if final else "ffn",
    )(x, x, x, sh, sc, g2, ng, wup, cw, cb, wd, fg)


def _rope_tables(n):
    pos = jnp.arange(n)
    row = (pos // GRID_W).astype(F32)
    col = (pos % GRID_W).astype(F32)
    axis_dim = HEAD_DIM // 2
    inv = ROPE_BASE ** (-jnp.arange(0, axis_dim, 2, dtype=F32) / axis_dim)
    ang_r = row[:, None] * inv[None, :]
    ang_c = col[:, None] * inv[None, :]
    cr, sr, cc, sn = jnp.cos(ang_r), jnp.sin(ang_r), jnp.cos(ang_c), jnp.sin(ang_c)
    cos_h = jnp.concatenate([cr, cr, cc, cc], axis=1)
    sin_h = jnp.concatenate([-sr, sr, -sn, sn], axis=1)
    return jnp.tile(cos_h, (1, 2)), jnp.tile(sin_h, (1, 2))


def _pool_inv_count(l):
    t = np.arange(l)
    cols = []
    for w in POOL_WINDOWS:
        left = w // 2
        right = w - 1 - left
        cnt = (np.minimum(t + right, l - 1) - np.maximum(t - left, 0) + 1).astype(np.float32)
        cols.append(np.repeat((np.float32(1.0) / cnt)[:, None], POOL_GROUP_DIM, axis=1))
    return jnp.asarray(np.concatenate(cols, axis=1))


def _block_diag(w):
    g, a, b = w.shape
    out = jnp.zeros((g * a, g * b), w.dtype)
    for i in range(g):
        out = out.at[i * a:(i + 1) * a, i * b:(i + 1) * b].set(w[i])
    return out


def _head_mean_matrix(n_heads):
    n = n_heads * HEAD_DIM
    idx = np.arange(n) // HEAD_DIM
    return jnp.asarray((idx[:, None] == idx[None, :]).astype(np.float32) / HEAD_DIM, dtype=BF16)


def kernel(x, c, ctx, c_ctx, w_mod, b_mod, norm1_g, w_in, q_gain, k_gain, w_s, b_s, w_pool,
           pool_scale, w_out, norm2_g, w_up, conv_w, conv_b, w_down, final_g):
    b, s, d = x.shape
    cl = ctx.shape[1]
    depth = w_mod.shape[0]
    d_ff = w_down.shape[1]
    fc = FFN_CHUNK
    nch = d_ff // fc

    rows = ((b + 1 + SUBLANES - 1) // SUBLANES) * SUBLANES
    cvec = jnp.zeros((rows, d), F32).at[:b].set(c).at[b].set(c_ctx)
    mod = _mod_call(cvec, w_mod, b_mod)

    cos_x, sin_x = _rope_tables(s)
    cos_c = jnp.ones((cl, LANES), F32)
    sin_c = jnp.zeros((cl, LANES), F32)
    e_q = _head_mean_matrix(N_HEADS)
    e_k = _head_mean_matrix(N_KV_HEADS)
    icnt_x = _pool_inv_count(s)
    icnt_c = _pool_inv_count(cl)
    fg = final_g.reshape(1, d)

    tile_c = min(256, cl)
    big_x = min(512, s)
    wide_x = min(1024, s)

    for i in range(depth):
        last = i == depth - 1
        m6 = [mod[i, j].reshape(rows, 1, d) for j in range(6)]
        g1n = norm1_g[i].reshape(1, d)
        g2n = norm2_g[i].reshape(1, d)
        w_in_b = w_in[i].astype(BF16)
        w_out_b = w_out[i].astype(BF16)
        gain = jnp.concatenate([jnp.tile(q_gain[i] * (HEAD_DIM ** -0.5 * math.log2(math.e)), N_HEADS),
                                jnp.tile(k_gain[i], N_KV_HEADS)]).reshape(1, -1)
        ws_b = w_s[i].astype(BF16)
        bias = jnp.repeat(b_s[i].T, d // 4 // N_SGU_GROUPS, axis=1)
        wp_bd = _block_diag(w_pool[i]).astype(BF16)
        pscale = pool_scale[i].reshape(1, -1)
        wup = w_up[i].astype(BF16)
        cw = conv_w[i]
        cb = conv_b[i].reshape(1, -1)
        wd = w_down[i].astype(BF16).reshape(nch, fc, d)

        qx, kx, vtx, ux, svx, xpx = _proj_call(x, m6[0], m6[1], g1n, w_in_b, e_q, e_k, gain,
                                               cos_x, sin_x, wide_x)
        if last:
            kc, vtc = _proj_call(ctx, m6[0], m6[1], g1n, w_in_b, e_q, e_k, gain,
                                 cos_c, sin_c, tile_c, cond_row=b, kv_only=True)
        else:
            qc, kc, vtc, uc, svc, xpc = _proj_call(ctx, m6[0], m6[1], g1n, w_in_b, e_q, e_k, gain,
                                                   cos_c, sin_c, tile_c, cond_row=b)
        score_bound = (1.01 * HEAD_DIM * jnp.max(jnp.abs(gain[0, :N_HEADS * HEAD_DIM]))
                       * jnp.max(jnp.abs(k_gain[i])))
        att_x = _attn_call(qx, [kx, kc], [vtx, vtc], wide_x, score_bound)
        x = _mix_call(att_x, ux, svx, xpx, x, m6[2], ws_b, bias, wp_bd, pscale, icnt_x,
                      w_out_b, wide_x)
        x = _ffn_call(x, m6[3], m6[4], m6[5], g2n, wup, cw, cb, wd, fg,
                      big_x, last)
        if not last:
            att_c = _attn_call(qc, [kc], [vtc], tile_c)
            ctx = _mix_call(att_c, uc, svc, xpc, ctx, m6[2], ws_b, bias, wp_bd, pscale, icnt_c,
                            w_out_b, tile_c, cond_row=b)
            ctx = _ffn_call(ctx, m6[3], m6[4], m6[5], g2n, wup, cw, cb, wd, fg,
                            tile_c, False, cond_row=b)
    return x
```

```python
import functools
import math

import numpy as np
import jax
import jax.numpy as jnp
from jax import lax
from jax.experimental import pallas as pl
from jax.experimental.pallas import tpu as pltpu

F32 = jnp.float32
BF16 = jnp.bfloat16

EPS = 1e-6
GRID_W = 64
HEAD_DIM = 64
N_HEADS = 8
N_KV_HEADS = 2
ROPE_BASE = 10000.0
CHUNK = 128
N_SGU_GROUPS = 4
POOL_WINDOWS = (2, 4, 8, 16)
POOL_GROUP_DIM = 64
LANES = 128
SUBLANES = 8
HALO = SUBLANES
VMEM_LIMIT = 56 * 1024 * 1024
FFN_CHUNK = 256
FFN_Z_SLOTS = 4
ATTN_SLOTS = 2
ATTN_SUB = 256
ATTN_KEY_CHUNK = 1024
ATTN_MAX_SHIFTED_RANGE = 100.0
PROJ_SUB = 256


def _cparams(n_axes):
    return pltpu.CompilerParams(
        dimension_semantics=("arbitrary",) * n_axes, vmem_limit_bytes=VMEM_LIMIT)


def _const_spec(shape):
    nd = len(shape)
    return pl.BlockSpec(shape, lambda *_: (0,) * nd, pipeline_mode=pl.Buffered(1))


def _silu(x):
    return x * (1.0 / (1.0 + jnp.exp(-x)))


def _rms_mod(x, g, shift, scale):
    ms = jnp.mean(x * x, axis=-1, keepdims=True)
    y = x * lax.rsqrt(ms + EPS) * g
    return y * (1.0 + scale) + shift


def _mod_kernel(c_ref, w_ref, b_ref, o_ref):
    s = _silu(c_ref[...]).astype(BF16)
    o_ref[0, 0] = jnp.dot(s, w_ref[0].astype(BF16), preferred_element_type=F32) + b_ref[0]


def _mod_call(cvec, w_mod, b_mod):
    depth, d, n = w_mod.shape
    rows = cvec.shape[0]
    return pl.pallas_call(
        _mod_kernel,
        grid=(depth, n // d),
        in_specs=[
            pl.BlockSpec((rows, d), lambda i, j: (0, 0)),
            pl.BlockSpec((1, d, d), lambda i, j: (i, 0, j)),
            pl.BlockSpec((1, 1, d), lambda i, j: (i, 0, j)),
        ],
        out_specs=pl.BlockSpec((1, 1, rows, d), lambda i, j: (i, j, 0, 0)),
        out_shape=jax.ShapeDtypeStruct((depth, n // d, rows, d), F32),
        compiler_params=_cparams(2),
        name="mod",
    )(cvec, w_mod, b_mod.reshape(depth, 1, n))


def _proj_kernel(x_ref, sh_ref, sc_ref, g_ref, w_ref, eq_ref, ek_ref, gain_ref, cos_ref, sin_ref,
                 *out_refs, kv_only):
    if kv_only:
        k_ref, vt_ref = out_refs
    else:
        q_ref, k_ref, vt_ref, u_ref, sv_ref, xp_ref = out_refs
    t = x_ref.shape[1]
    sub = min(t, PROJ_SUB)
    nq = N_HEADS * HEAD_DIM
    nqk = (N_HEADS + N_KV_HEADS) * HEAD_DIM
    nv = N_KV_HEADS * HEAD_DIM
    col0 = nq if kv_only else 0
    lane = lax.broadcasted_iota(jnp.int32, (sub, LANES), 1)
    first = (lane % 32) < 16
    lo = lane < HEAD_DIM

    def project(r):
        rows = slice(sub * r, sub * (r + 1))
        h = _rms_mod(x_ref[0, rows], g_ref[...], sh_ref[0], sc_ref[0]).astype(BF16)
        w = w_ref[:, col0:nqk + nv] if kv_only else w_ref[...]
        return jnp.dot(h, w, preferred_element_type=F32)

    def finish(r, p):
        rows = slice(sub * r, sub * (r + 1))
        qk = p[:, :nqk - col0]
        sq = (qk * qk).astype(BF16)
        msk = jnp.dot(sq[:, nq - col0:], ek_ref[...], preferred_element_type=F32)
        if kv_only:
            msh = msk
        else:
            msh = jnp.concatenate(
                [jnp.dot(sq[:, :nq], eq_ref[...], preferred_element_type=F32), msk], axis=1)
        qkn = qk * lax.rsqrt(msh + EPS) * gain_ref[:, col0:]
        c = cos_ref[rows]
        s = sin_ref[rows]
        for j in range(col0 // LANES, nqk // LANES):
            blk = qkn[:, LANES * j - col0:LANES * (j + 1) - col0]
            sw = jnp.where(first, pltpu.roll(blk, LANES - 16, axis=1), pltpu.roll(blk, 16, axis=1))
            rot = blk * c + sw * s
            if j < N_HEADS // 2:
                kvh = (2 * j) // (N_HEADS // N_KV_HEADS)
                rr = pltpu.roll(rot, HEAD_DIM, axis=1)
                keep = lo if kvh == 0 else jnp.logical_not(lo)
                even = rot if kvh == 0 else rr
                odd = rr if kvh == 0 else rot
                q_ref[0, rows, LANES * (2 * j):LANES * (2 * j + 1)] = (
                    jnp.where(keep, even, 0.0).astype(BF16))
                q_ref[0, rows, LANES * (2 * j + 1):LANES * (2 * j + 2)] = (
                    jnp.where(keep, odd, 0.0).astype(BF16))
            else:
                k_ref[0, rows] = rot.astype(BF16)
        vt_ref[0, :, rows] = p[:, nqk - col0:nqk - col0 + nv].T.astype(BF16)
        if not kv_only:
            u_ref[0, rows] = p[:, 768:1024]
            sv_ref[0, rows] = p[:, 1024:1280].astype(BF16)
            xp_ref[0, rows] = p[:, 1280:1536]

    nsub = t // sub
    p_next = project(0)
    for r in range(nsub):
        p = p_next
        if r + 1 < nsub:
            p_next = project(r + 1)
        finish(r, p)


def _cond_index(cond_row):
    return (lambda b_: b_) if cond_row is None else (lambda b_: cond_row)


def _proj_call(x, sh, sc, g, w_in, e_q, e_k, gain, cos_t, sin_t, tile, cond_row=None,
               kv_only=False):
    b, l, d = x.shape
    nt = l // tile
    n_in = w_in.shape[1]
    crow = _cond_index(cond_row)
    row = lambda t_, b_: (b_, t_, 0)
    vec = lambda t_, b_: (crow(b_), 0, 0)
    tab = lambda t_, b_: (t_, 0)
    out_specs = [
        pl.BlockSpec((1, tile, N_HEADS * LANES), row),
        pl.BlockSpec((1, tile, LANES), row),
        pl.BlockSpec((1, LANES, tile), lambda t_, b_: (b_, 0, t_)),
        pl.BlockSpec((1, tile, 256), row),
        pl.BlockSpec((1, tile, 256), row),
        pl.BlockSpec((1, tile, 256), row),
    ]
    out_shape = [
        jax.ShapeDtypeStruct((b, l, N_HEADS * LANES), BF16),
        jax.ShapeDtypeStruct((b, l, LANES), BF16),
        jax.ShapeDtypeStruct((b, LANES, l), BF16),
        jax.ShapeDtypeStruct((b, l, 256), F32),
        jax.ShapeDtypeStruct((b, l, 256), BF16),
        jax.ShapeDtypeStruct((b, l, 256), F32),
    ]
    if kv_only:
        out_specs, out_shape = out_specs[1:3], out_shape[1:3]
    return pl.pallas_call(
        functools.partial(_proj_kernel, kv_only=kv_only),
        grid=(nt, b),
        in_specs=[
            pl.BlockSpec((1, tile, d), row),
            pl.BlockSpec((1, 1, d), vec),
            pl.BlockSpec((1, 1, d), vec),
            _const_spec((1, d)),
            _const_spec((d, n_in)),
            _const_spec(e_q.shape),
            _const_spec(e_k.shape),
            _const_spec(gain.shape),
            pl.BlockSpec((tile, LANES), tab),
            pl.BlockSpec((tile, LANES), tab),
        ],
        out_specs=out_specs,
        out_shape=out_shape,
        compiler_params=_cparams(2),
        name="proj_kv" if kv_only else "proj",
    )(x, sh, sc, g, w_in, e_q, e_k, gain, cos_t, sin_t)


def _attn_kernel(*refs, n_kv):
    q_ref = refs[0]
    k_refs = refs[1:1 + n_kv]
    vt_refs = refs[1 + n_kv:1 + 2 * n_kv]
    o_ref, s_ref, p_ref = refs[1 + 2 * n_kv:]
    group = N_HEADS // N_KV_HEADS
    nslot = s_ref.shape[0]
    sub = s_ref.shape[2]
    nunit = (q_ref.shape[1] // sub) * N_HEADS

    chunks = []
    base = 0
    for j, kr in enumerate(k_refs):
        n = kr.shape[1]
        step = min(n, ATTN_KEY_CHUNK)
        chunks += [(j, r0, step, base + r0) for r0 in range(0, n, step)]
        base += n

    def q_unit(u):
        r, i = divmod(u, N_HEADS)
        return q_ref[0, sub * r:sub * (r + 1), LANES * i:LANES * (i + 1)]

    pending = []

    def emit(u, ot, l):
        r, i = divmod(u, N_HEADS)
        kvh = i // group
        pending.append(ot[HEAD_DIM * kvh:HEAD_DIM * (kvh + 1)] * (1.0 / l))
        if i % 2 == 1:
            pair = jnp.concatenate(pending, axis=0)
            pending.clear()
            o_ref[0, sub * r:sub * (r + 1), LANES * (i // 2):LANES * (i // 2 + 1)] = (
                pair.T.astype(BF16))

    m = None
    l_prev = None
    for u in range(-1, nunit + 1):
        do_scores = u + 1 < nunit
        do_softmax = 0 <= u < nunit
        do_pv = 1 <= u
        q_next = q_unit(u + 1) if do_scores else None
        mparts, lparts, ot = [], [], None
        for j, r0, n, s0 in chunks:
            if do_scores:
                st = lax.dot_general(k_refs[j][0, r0:r0 + n], q_next, (((1,), (1,)), ((), ())),
                                     preferred_element_type=F32)
                s_ref[(u + 1) % nslot, s0:s0 + n] = st
                mparts.append(jnp.max(st, axis=0, keepdims=True))
            if do_softmax:
                p = jnp.exp2(s_ref[u % nslot, s0:s0 + n] - m)
                lparts.append(jnp.sum(p, axis=0, keepdims=True))
                p_ref[u % nslot, s0:s0 + n] = p.astype(BF16)
            if do_pv:
                d = jnp.dot(vt_refs[j][0, :, r0:r0 + n], p_ref[(u - 1) % nslot, s0:s0 + n],
                            preferred_element_type=F32)
                ot = d if ot is None else ot + d
        if do_pv:
            emit(u - 1, ot, l_prev)
        if do_softmax:
            l_prev = functools.reduce(jnp.add, lparts)
        if do_scores:
            m = functools.reduce(jnp.maximum, mparts)


def _attn_bounded_kernel(*refs, n_kv):
    q_ref = refs[0]
    k_refs = refs[1:1 + n_kv]
    vt_refs = refs[1 + n_kv:1 + 2 * n_kv]
    shift_ref, o_ref, p_ref = refs[1 + 2 * n_kv:]
    group = N_HEADS // N_KV_HEADS
    nslot = p_ref.shape[0]
    sub = p_ref.shape[2]
    nunit = (q_ref.shape[1] // sub) * N_HEADS
    shift = shift_ref[0]

    chunks = []
    base = 0
    for j, kr in enumerate(k_refs):
        n = kr.shape[1]
        step = min(n, ATTN_KEY_CHUNK)
        chunks += [(j, r0, step, base + r0) for r0 in range(0, n, step)]
        base += n

    def q_unit(u):
        r, i = divmod(u, N_HEADS)
        return q_ref[0, sub * r:sub * (r + 1), LANES * i:LANES * (i + 1)]

    def v_ext(kvh, j, r0, n):
        vt = vt_refs[j][0, HEAD_DIM * kvh:HEAD_DIM * (kvh + 1), r0:r0 + n]
        return jnp.concatenate([vt, jnp.ones((2 * SUBLANES, n), BF16)], axis=0)

    pending = []

    def emit(u, ot):
        r, i = divmod(u, N_HEADS)
        pending.append(ot[:HEAD_DIM] * (1.0 / ot[HEAD_DIM:HEAD_DIM + 1]))
        if i % 2 == 1:
            pair = jnp.concatenate(pending, axis=0)
            pending.clear()
            o_ref[0, sub * r:sub * (r + 1), LANES * (i // 2):LANES * (i // 2 + 1)] = (
                pair.T.astype(BF16))

    for u in range(nunit + 1):
        q_u = q_unit(u) if u < nunit else None
        kvh_prev = ((u - 1) % N_HEADS) // group
        ot = None
        for j, r0, n, s0 in chunks:
            if u < nunit:
                st = lax.dot_general(k_refs[j][0, r0:r0 + n], q_u, (((1,), (1,)), ((), ())),
                                     preferred_element_type=F32)
                p_ref[u % nslot, s0:s0 + n] = jnp.exp2(st - shift).astype(BF16)
            if u >= 1:
                d = jnp.dot(v_ext(kvh_prev, j, r0, n), p_ref[(u - 1) % nslot, s0:s0 + n],
                            preferred_element_type=F32)
                ot = d if ot is None else ot + d
        if u >= 1:
            emit(u - 1, ot)


def _attn_call(q, ks, vts, tq, score_bound=None):
    b, s, _ = q.shape
    n_kv = len(ks)
    nk = sum(k.shape[1] for k in ks)
    sub = min(tq, ATTN_SUB)
    kv_map = lambda b_, t_: (b_, 0, 0)
    in_specs = ([pl.BlockSpec((1, tq, N_HEADS * LANES), lambda b_, t_: (b_, t_, 0))]
                + [pl.BlockSpec((1, k.shape[1], LANES), kv_map) for k in ks]
                + [pl.BlockSpec((1, LANES, vt.shape[2]), kv_map) for vt in vts])
    common = dict(
        grid=(b, s // tq),
        out_specs=pl.BlockSpec((1, tq, N_HEADS * HEAD_DIM), lambda b_, t_: (b_, t_, 0)),
        out_shape=jax.ShapeDtypeStruct((b, s, N_HEADS * HEAD_DIM), BF16),
        compiler_params=_cparams(2),
    )

    def exact(_):
        return pl.pallas_call(
            functools.partial(_attn_kernel, n_kv=n_kv),
            in_specs=in_specs,
            scratch_shapes=[pltpu.VMEM((ATTN_SLOTS, nk, sub), F32),
                            pltpu.VMEM((ATTN_SLOTS, nk, sub), BF16)],
            name="attn", **common,
        )(q, *ks, *vts)

    if score_bound is None:
        return exact(None)

    def bounded(shift):
        return pl.pallas_call(
            functools.partial(_attn_bounded_kernel, n_kv=n_kv),
            in_specs=in_specs + [pl.BlockSpec(memory_space=pltpu.SMEM)],
            scratch_shapes=[pltpu.VMEM((ATTN_SLOTS, nk, sub), BF16)],
            name="attn_bounded", **common,
        )(q, *ks, *vts, shift.reshape(1))

    return lax.cond(2.0 * score_bound <= ATTN_MAX_SHIFTED_RANGE, bounded, exact, score_bound)


def _mix_kernel(att_ref, u_ref, sv_ref, xp_ref, xpp_ref, xpn_ref, x_ref, g1_ref,
                ws_ref, bias_ref, wp_ref, ps_ref, icnt_ref, wo_ref, o_ref):
    t = x_ref.shape[1]
    ti = pl.program_id(1)
    nt = pl.num_programs(1)
    lane = lax.broadcasted_iota(jnp.int32, (CHUNK, LANES), 1)
    lo = lane < 64

    sv = sv_ref[0]
    rows = []
    for n in range(t // CHUNK):
        cols = []
        for jp in range(N_SGU_GROUPS // 2):
            blk = sv[CHUNK * n:CHUNK * (n + 1), LANES * jp:LANES * (jp + 1)]
            r0 = jnp.dot(ws_ref[2 * jp], blk, preferred_element_type=F32)
            r1 = jnp.dot(ws_ref[2 * jp + 1], blk, preferred_element_type=F32)
            cols.append(jnp.where(lo, r0, r1))
        rows.append(jnp.concatenate(cols, axis=1) + bias_ref[...])
    sgu = u_ref[0] * jnp.concatenate(rows, axis=0)

    xp = xp_ref[0]
    prev = jnp.where(ti > 0, xpp_ref[0], 0.0)
    nxt = jnp.where(ti < nt - 1, xpn_ref[0], 0.0)
    xe = jnp.concatenate([prev, xp, nxt], axis=0)
    lane_t = lax.broadcasted_iota(jnp.int32, (t, LANES), 1)
    lo_t = lane_t < 64

    def shifted(col, k_):
        return xe[HALO + k_:HALO + k_ + t, LANES * col:LANES * (col + 1)]

    def window(col, w, inner=None, inner_w=0):
        left = w // 2
        right = w - 1 - left
        ileft = inner_w // 2
        iright = inner_w - 1 - ileft
        acc = inner
        for k_ in range(-left, right + 1):
            if inner is not None and -ileft <= k_ <= iright:
                continue
            term = shifted(col, k_)
            acc = term if acc is None else acc + term
        return acc

    w2 = window(0, 2)
    w4 = window(0, 4, w2, 2)
    w8 = window(1, 8)
    w16 = window(1, 16, w8, 8)
    tot = jnp.concatenate([jnp.where(lo_t, w2, w4), jnp.where(lo_t, w8, w16)], axis=1)
    y = tot * icnt_ref[...] - xp
    pool = jnp.dot(y.astype(BF16), wp_ref[...], preferred_element_type=F32) * ps_ref[...]

    na = att_ref.shape[2]
    ns = sgu.shape[1]
    mix = jnp.dot(att_ref[0], wo_ref[0:na, :], preferred_element_type=F32)
    mix = mix + jnp.dot(sgu.astype(BF16), wo_ref[na:na + ns, :], preferred_element_type=F32)
    mix = mix + jnp.dot(pool.astype(BF16), wo_ref[na + ns:, :], preferred_element_type=F32)
    o_ref[0] = x_ref[0] + g1_ref[0] * mix


def _mix_call(att, u, sv, xp, x, g1, ws, bias, wp_bd, pscale, icnt, w_out, tile, cond_row=None):
    b, l, d = x.shape
    nt = l // tile
    hb = tile // HALO
    nhb = l // HALO
    crow = _cond_index(cond_row)
    row = lambda b_, t_: (b_, t_, 0)
    vec = lambda b_, t_: (crow(b_), 0, 0)
    prev = lambda b_, t_: (b_, jnp.maximum(t_ * hb - 1, 0), 0)
    nxt = lambda b_, t_: (b_, jnp.minimum((t_ + 1) * hb, nhb - 1), 0)
    return pl.pallas_call(
        _mix_kernel,
        grid=(b, nt),
        in_specs=[
            pl.BlockSpec((1, tile, att.shape[2]), row),
            pl.BlockSpec((1, tile, 256), row),
            pl.BlockSpec((1, tile, 256), row),
            pl.BlockSpec((1, tile, 256), row),
            pl.BlockSpec((1, HALO, 256), prev),
            pl.BlockSpec((1, HALO, 256), nxt),
            pl.BlockSpec((1, tile, d), row),
            pl.BlockSpec((1, 1, d), vec),
            _const_spec(ws.shape),
            _const_spec(bias.shape),
            _const_spec(wp_bd.shape),
            _const_spec(pscale.shape),
            pl.BlockSpec((tile, 256), lambda b_, t_: (t_, 0)),
            _const_spec(w_out.shape),
        ],
        out_specs=pl.BlockSpec((1, tile, d), row),
        out_shape=jax.ShapeDtypeStruct((b, l, d), F32),
        compiler_params=_cparams(2),
        name="mix",
    )(att, u, sv, xp, xp, xp, x, g1, ws, bias, wp_bd, pscale, icnt, w_out)


def _ffn_kernel(x_ref, xp_ref, xn_ref, sh_ref, sc_ref, g2_ref, ng_ref, wup_ref, cw_ref, cb_ref,
                wd_ref, fg_ref, o_ref, z_ref, *, final):
    t = x_ref.shape[1]
    ti = pl.program_id(1)
    nt = pl.num_programs(1)
    x = x_ref[0]
    ng = ng_ref[...]
    sh = sh_ref[0]
    sc = sc_ref[0]
    hprev = jnp.where(ti > 0, _rms_mod(xp_ref[0], ng, sh, sc), 0.0)
    hnext = jnp.where(ti < nt - 1, _rms_mod(xn_ref[0], ng, sh, sc), 0.0)
    h = jnp.concatenate([hprev, _rms_mod(x, ng, sh, sc), hnext], axis=0).astype(BF16)
    nch, fc, _ = wd_ref.shape
    d_ff = nch * fc

    nl = fc // LANES
    nslot = z_ref.shape[0]

    def up(c):
        slot = c % nslot
        zg = jnp.dot(h, wup_ref[:, fc * c:fc * (c + 1)], preferred_element_type=F32)
        zv = jnp.dot(h, wup_ref[:, d_ff + fc * c:d_ff + fc * (c + 1)], preferred_element_type=F32)
        for j in range(nl):
            z_ref[slot, j] = zg[:, LANES * j:LANES * (j + 1)]
            z_ref[slot, nl + j] = zv[:, LANES * j:LANES * (j + 1)]

    def conv(slot, j, cw, cb):
        return (z_ref[slot, j, HALO - 1:HALO - 1 + t] * cw[0:1]
                + z_ref[slot, j, HALO:HALO + t] * cw[1:2]
                + z_ref[slot, j, HALO + 1:HALO + 1 + t] * cw[2:3] + cb)

    acc = jnp.zeros((t, x.shape[1]), F32)
    for c in range(min(nslot - 1, nch)):
        up(c)
    for c in range(nch):
        if c + nslot - 1 < nch:
            up(c + nslot - 1)
        slot = c % nslot
        parts = []
        for j in range(nl):
            lg = slice(fc * c + LANES * j, fc * c + LANES * (j + 1))
            lv = slice(d_ff + fc * c + LANES * j, d_ff + fc * c + LANES * (j + 1))
            cg = conv(slot, j, cw_ref[:, lg], cb_ref[:, lg])
            cv = conv(slot, nl + j, cw_ref[:, lv], cb_ref[:, lv])
            parts.append((_silu(cg) * cv).astype(BF16))
        a = jnp.concatenate(parts, axis=1)
        acc = acc + jnp.dot(a, wd_ref[c], preferred_element_type=F32)
    y = x + g2_ref[0] * acc
    if final:
        ms = jnp.mean(y * y, axis=-1, keepdims=True)
        y = y * lax.rsqrt(ms + EPS) * fg_ref[...]
    o_ref[0] = y


def _ffn_call(x, sh, sc, g2, ng, wup, cw, cb, wd, fg, tile, final, cond_row=None):
    b, l, d = x.shape
    nt = l // tile
    hb = tile // HALO
    nhb = l // HALO
    crow = _cond_index(cond_row)
    row = lambda b_, t_: (b_, t_, 0)
    vec = lambda b_, t_: (crow(b_), 0, 0)
    prev = lambda b_, t_: (b_, jnp.maximum(t_ * hb - 1, 0), 0)
    nxt = lambda b_, t_: (b_, jnp.minimum((t_ + 1) * hb, nhb - 1), 0)
    return pl.pallas_call(
        functools.partial(_ffn_kernel, final=final),
        grid=(b, nt),
        in_specs=[
            pl.BlockSpec((1, tile, d), row),
            pl.BlockSpec((1, HALO, d), prev),
            pl.BlockSpec((1, HALO, d), nxt),
            pl.BlockSpec((1, 1, d), vec),
            pl.BlockSpec((1, 1, d), vec),
            pl.BlockSpec((1, 1, d), vec),
            _const_spec(ng.shape),
            _const_spec(wup.shape),
            _const_spec(cw.shape),
            _const_spec(cb.shape),
            _const_spec(wd.shape),
            _const_spec(fg.shape),
        ],
        out_specs=pl.BlockSpec((1, tile, d), row),
        out_shape=jax.ShapeDtypeStruct((b, l, d), F32),
        scratch_shapes=[pltpu.VMEM(
            (FFN_Z_SLOTS, 2 * wd.shape[1] // LANES, tile + 2 * HALO, LANES), F32)],
        compiler_params=_cparams(2),
        name="ffn_final" if final else "ffn",
    )(x, x, x, sh, sc, g2, ng, wup, cw, cb, wd, fg)


def _rope_tables(n):
    pos = jnp.arange(n)
    row = (pos // GRID_W).astype(F32)
    col = (pos % GRID_W).astype(F32)
    axis_dim = HEAD_DIM // 2
    inv = ROPE_BASE ** (-jnp.arange(0, axis_dim, 2, dtype=F32) / axis_dim)
    ang_r = row[:, None] * inv[None, :]
    ang_c = col[:, None] * inv[None, :]
    cr, sr, cc, sn = jnp.cos(ang_r), jnp.sin(ang_r), jnp.cos(ang_c), jnp.sin(ang_c)
    cos_h = jnp.concatenate([cr, cr, cc, cc], axis=1)
    sin_h = jnp.concatenate([-sr, sr, -sn, sn], axis=1)
    return jnp.tile(cos_h, (1, 2)), jnp.tile(sin_h, (1, 2))


def _pool_inv_count(l):
    t = np.arange(l)
    cols = []
    for w in POOL_WINDOWS:
        left = w // 2
        right = w - 1 - left
        cnt = (np.minimum(t + right, l - 1) - np.maximum(t - left, 0) + 1).astype(np.float32)
        cols.append(np.repeat((np.float32(1.0) / cnt)[:, None], POOL_GROUP_DIM, axis=1))
    return jnp.asarray(np.concatenate(cols, axis=1))


def _block_diag(w):
    g, a, b = w.shape
    out = jnp.zeros((g * a, g * b), w.dtype)
    for i in range(g):
        out = out.at[i * a:(i + 1) * a, i * b:(i + 1) * b].set(w[i])
    return out


def _head_mean_matrix(n_heads):
    n = n_heads * HEAD_DIM
    idx = np.arange(n) // HEAD_DIM
    return jnp.asarray((idx[:, None] == idx[None, :]).astype(np.float32) / HEAD_DIM, dtype=BF16)


def kernel(x, c, ctx, c_ctx, w_mod, b_mod, norm1_g, w_in, q_gain, k_gain, w_s, b_s, w_pool,
           pool_scale, w_out, norm2_g, w_up, conv_w, conv_b, w_down, final_g):
    b, s, d = x.shape
    cl = ctx.shape[1]
    depth = w_mod.shape[0]
    d_ff = w_down.shape[1]
    fc = FFN_CHUNK
    nch = d_ff // fc

    rows = ((b + 1 + SUBLANES - 1) // SUBLANES) * SUBLANES
    cvec = jnp.zeros((rows, d), F32).at[:b].set(c).at[b].set(c_ctx)
    mod = _mod_call(cvec, w_mod, b_mod)

    cos_x, sin_x = _rope_tables(s)
    cos_c = jnp.ones((cl, LANES), F32)
    sin_c = jnp.zeros((cl, LANES), F32)
    e_q = _head_mean_matrix(N_HEADS)
    e_k = _head_mean_matrix(N_KV_HEADS)
    icnt_x = _pool_inv_count(s)
    icnt_c = _pool_inv_count(cl)
    fg = final_g.reshape(1, d)

    tile_c = min(256, cl)
    big_x = min(512, s)
    wide_x = min(1024, s)
    proj_x = min(2048, s)

    for i in range(depth):
        last = i == depth - 1
        m6 = [mod[i, j].reshape(rows, 1, d) for j in range(6)]
        g1n = norm1_g[i].reshape(1, d)
        g2n = norm2_g[i].reshape(1, d)
        w_in_b = w_in[i].astype(BF16)
        w_out_b = w_out[i].astype(BF16)
        gain = jnp.concatenate([jnp.tile(q_gain[i] * (HEAD_DIM ** -0.5 * math.log2(math.e)), N_HEADS),
                                jnp.tile(k_gain[i], N_KV_HEADS)]).reshape(1, -1)
        ws_b = w_s[i].astype(BF16)
        bias = jnp.repeat(b_s[i].T, d // 4 // N_SGU_GROUPS, axis=1)
        wp_bd = _block_diag(w_pool[i]).astype(BF16)
        pscale = pool_scale[i].reshape(1, -1)
        wup = w_up[i].astype(BF16)
        cw = conv_w[i]
        cb = conv_b[i].reshape(1, -1)
        wd = w_down[i].astype(BF16).reshape(nch, fc, d)

        qx, kx, vtx, ux, svx, xpx = _proj_call(x, m6[0], m6[1], g1n, w_in_b, e_q, e_k, gain,
                                               cos_x, sin_x, proj_x)
        if last:
            kc, vtc = _proj_call(ctx, m6[0], m6[1], g1n, w_in_b, e_q, e_k, gain,
                                 cos_c, sin_c, tile_c, cond_row=b, kv_only=True)
        else:
            qc, kc, vtc, uc, svc, xpc = _proj_call(ctx, m6[0], m6[1], g1n, w_in_b, e_q, e_k, gain,
                                                   cos_c, sin_c, tile_c, cond_row=b)
        score_bound = (1.01 * HEAD_DIM * jnp.max(jnp.abs(gain[0, :N_HEADS * HEAD_DIM]))
                       * jnp.max(jnp.abs(k_gain[i])))
        att_x = _attn_call(qx, [kx, kc], [vtx, vtc], wide_x, score_bound)
        x = _mix_call(att_x, ux, svx, xpx, x, m6[2], ws_b, bias, wp_bd, pscale, icnt_x,
                      w_out_b, wide_x)
        x = _ffn_call(x, m6[3], m6[4], m6[5], g2n, wup, cw, cb, wd, fg,
                      big_x, last)
        if not last:
            att_c = _attn_call(qc, [kc], [vtc], tile_c)
            ctx = _mix_call(att_c, uc, svc, xpc, ctx, m6[2], ws_b, bias, wp_bd, pscale, icnt_c,
                            w_out_b, tile_c, cond_row=b)
            ctx = _ffn_call(ctx, m6[3], m6[4], m6[5], g2n, wup, cw, cb, wd, fg,
                            tile_c, False, cond_row=b)
    return x
```

```python
import functools
import math

import numpy as np
import jax
import jax.numpy as jnp
from jax import lax
from jax.experimental import pallas as pl
from jax.experimental.pallas import tpu as pltpu

F32 = jnp.float32
BF16 = jnp.bfloat16

EPS = 1e-6
GRID_W = 64
HEAD_DIM = 64
N_HEADS = 8
N_KV_HEADS = 2
ROPE_BASE = 10000.0
CHUNK = 128
N_SGU_GROUPS = 4
POOL_WINDOWS = (2, 4, 8, 16)
POOL_GROUP_DIM = 64
LANES = 128
SUBLANES = 8
HALO = SUBLANES
VMEM_LIMIT = 56 * 1024 * 1024
FFN_CHUNK = 256
FFN_Z_SLOTS = 4
ATTN_SLOTS = 2
ATTN_SUB = 256
ATTN_KEY_CHUNK = 1024
ATTN_MAX_SHIFTED_RANGE = 100.0
PROJ_SUB = 256


def _cparams(n_axes):
    return pltpu.CompilerParams(
        dimension_semantics=("arbitrary",) * n_axes, vmem_limit_bytes=VMEM_LIMIT)


def _const_spec(shape):
    nd = len(shape)
    return pl.BlockSpec(shape, lambda *_: (0,) * nd, pipeline_mode=pl.Buffered(1))


def _silu(x):
    return x * (1.0 / (1.0 + jnp.exp(-x)))


def _rms_mod(x, g, shift, scale):
    ms = jnp.mean(x * x, axis=-1, keepdims=True)
    y = x * lax.rsqrt(ms + EPS) * g
    return y * (1.0 + scale) + shift


def _mod_kernel(c_ref, w_ref, b_ref, o_ref):
    s = _silu(c_ref[...]).astype(BF16)
    o_ref[0, 0] = jnp.dot(s, w_ref[0].astype(BF16), preferred_element_type=F32) + b_ref[0]


def _mod_call(cvec, w_mod, b_mod):
    depth, d, n = w_mod.shape
    rows = cvec.shape[0]
    return pl.pallas_call(
        _mod_kernel,
        grid=(depth, n // d),
        in_specs=[
            pl.BlockSpec((rows, d), lambda i, j: (0, 0)),
            pl.BlockSpec((1, d, d), lambda i, j: (i, 0, j)),
            pl.BlockSpec((1, 1, d), lambda i, j: (i, 0, j)),
        ],
        out_specs=pl.BlockSpec((1, 1, rows, d), lambda i, j: (i, j, 0, 0)),
        out_shape=jax.ShapeDtypeStruct((depth, n // d, rows, d), F32),
        compiler_params=_cparams(2),
        name="mod",
    )(cvec, w_mod, b_mod.reshape(depth, 1, n))


def _proj_kernel(x_ref, sh_ref, sc_ref, g_ref, w_ref, eq_ref, ek_ref, gain_ref, cos_ref, sin_ref,
                 *out_refs, kv_only):
    if kv_only:
        k_ref, vt_ref = out_refs
    else:
        q_ref, k_ref, vt_ref, u_ref, sv_ref, xp_ref = out_refs
    t = x_ref.shape[1]
    sub = min(t, PROJ_SUB)
    nq = N_HEADS * HEAD_DIM
    nqk = (N_HEADS + N_KV_HEADS) * HEAD_DIM
    nv = N_KV_HEADS * HEAD_DIM
    col0 = nq if kv_only else 0
    lane = lax.broadcasted_iota(jnp.int32, (sub, LANES), 1)
    first = (lane % 32) < 16
    lo = lane < HEAD_DIM

    def project(r):
        rows = slice(sub * r, sub * (r + 1))
        h = _rms_mod(x_ref[0, rows], g_ref[...], sh_ref[0], sc_ref[0]).astype(BF16)
        w = w_ref[:, col0:nqk + nv] if kv_only else w_ref[...]
        return jnp.dot(h, w, preferred_element_type=F32)

    def finish(r, p):
        rows = slice(sub * r, sub * (r + 1))
        qk = p[:, :nqk - col0]
        sq = (qk * qk).astype(BF16)
        msk = jnp.dot(sq[:, nq - col0:], ek_ref[...], preferred_element_type=F32)
        if kv_only:
            msh = msk
        else:
            msh = jnp.concatenate(
                [jnp.dot(sq[:, :nq], eq_ref[...], preferred_element_type=F32), msk], axis=1)
        qkn = qk * lax.rsqrt(msh + EPS) * gain_ref[:, col0:]
        c = cos_ref[rows]
        s = sin_ref[rows]
        for j in range(col0 // LANES, nqk // LANES):
            blk = qkn[:, LANES * j - col0:LANES * (j + 1) - col0]
            sw = jnp.where(first, pltpu.roll(blk, LANES - 16, axis=1), pltpu.roll(blk, 16, axis=1))
            rot = blk * c + sw * s
            if j < N_HEADS // 2:
                kvh = (2 * j) // (N_HEADS // N_KV_HEADS)
                rr = pltpu.roll(rot, HEAD_DIM, axis=1)
                keep = lo if kvh == 0 else jnp.logical_not(lo)
                even = rot if kvh == 0 else rr
                odd = rr if kvh == 0 else rot
                q_ref[0, rows, LANES * (2 * j):LANES * (2 * j + 1)] = (
                    jnp.where(keep, even, 0.0).astype(BF16))
                q_ref[0, rows, LANES * (2 * j + 1):LANES * (2 * j + 2)] = (
                    jnp.where(keep, odd, 0.0).astype(BF16))
            else:
                k_ref[0, rows] = rot.astype(BF16)
        vt_ref[0, :, rows] = p[:, nqk - col0:nqk - col0 + nv].T.astype(BF16)
        if not kv_only:
            c_u = nqk + nv
            c_sv = c_u + u_ref.shape[2]
            c_xp = c_sv + sv_ref.shape[2]
            u_ref[0, rows] = p[:, c_u:c_sv]
            sv_ref[0, rows] = p[:, c_sv:c_xp].astype(BF16)
            xp_ref[0, rows] = p[:, c_xp:c_xp + xp_ref.shape[2]]

    nsub = t // sub
    p_next = project(0)
    for r in range(nsub):
        p = p_next
        if r + 1 < nsub:
            p_next = project(r + 1)
        finish(r, p)


def _cond_index(cond_row):
    return (lambda b_: b_) if cond_row is None else (lambda b_: cond_row)


def _proj_call(x, sh, sc, g, w_in, e_q, e_k, gain, cos_t, sin_t, tile, cond_row=None,
               kv_only=False):
    b, l, d = x.shape
    nt = l // tile
    n_in = w_in.shape[1]
    crow = _cond_index(cond_row)
    row = lambda t_, b_: (b_, t_, 0)
    vec = lambda t_, b_: (crow(b_), 0, 0)
    tab = lambda t_, b_: (t_, 0)
    wg3 = (n_in - (N_HEADS + 2 * N_KV_HEADS) * HEAD_DIM) // 3
    out_specs = [
        pl.BlockSpec((1, tile, N_HEADS * LANES), row),
        pl.BlockSpec((1, tile, LANES), row),
        pl.BlockSpec((1, LANES, tile), lambda t_, b_: (b_, 0, t_)),
        pl.BlockSpec((1, tile, wg3), row),
        pl.BlockSpec((1, tile, wg3), row),
        pl.BlockSpec((1, tile, wg3), row),
    ]
    out_shape = [
        jax.ShapeDtypeStruct((b, l, N_HEADS * LANES), BF16),
        jax.ShapeDtypeStruct((b, l, LANES), BF16),
        jax.ShapeDtypeStruct((b, LANES, l), BF16),
        jax.ShapeDtypeStruct((b, l, wg3), F32),
        jax.ShapeDtypeStruct((b, l, wg3), BF16),
        jax.ShapeDtypeStruct((b, l, wg3), F32),
    ]
    if kv_only:
        out_specs, out_shape = out_specs[1:3], out_shape[1:3]
    return pl.pallas_call(
        functools.partial(_proj_kernel, kv_only=kv_only),
        grid=(nt, b),
        in_specs=[
            pl.BlockSpec((1, tile, d), row),
            pl.BlockSpec((1, 1, d), vec),
            pl.BlockSpec((1, 1, d), vec),
            _const_spec((1, d)),
            _const_spec((d, n_in)),
            _const_spec(e_q.shape),
            _const_spec(e_k.shape),
            _const_spec(gain.shape),
            pl.BlockSpec((tile, LANES), tab),
            pl.BlockSpec((tile, LANES), tab),
        ],
        out_specs=out_specs,
        out_shape=out_shape,
        compiler_params=_cparams(2),
        name="proj_kv" if kv_only else "proj",
    )(x, sh, sc, g, w_in, e_q, e_k, gain, cos_t, sin_t)


def _attn_kernel(*refs, n_kv):
    q_ref = refs[0]
    k_refs = refs[1:1 + n_kv]
    vt_refs = refs[1 + n_kv:1 + 2 * n_kv]
    o_ref, s_ref, p_ref = refs[1 + 2 * n_kv:]
    group = N_HEADS // N_KV_HEADS
    nslot = s_ref.shape[0]
    sub = s_ref.shape[2]
    nunit = (q_ref.shape[1] // sub) * N_HEADS

    chunks = []
    base = 0
    for j, kr in enumerate(k_refs):
        n = kr.shape[1]
        step = min(n, ATTN_KEY_CHUNK)
        chunks += [(j, r0, step, base + r0) for r0 in range(0, n, step)]
        base += n

    def q_unit(u):
        r, i = divmod(u, N_HEADS)
        return q_ref[0, sub * r:sub * (r + 1), LANES * i:LANES * (i + 1)]

    pending = []

    def emit(u, ot, l):
        r, i = divmod(u, N_HEADS)
        kvh = i // group
        pending.append(ot[HEAD_DIM * kvh:HEAD_DIM * (kvh + 1)] * (1.0 / l))
        if i % 2 == 1:
            pair = jnp.concatenate(pending, axis=0)
            pending.clear()
            o_ref[0, sub * r:sub * (r + 1), LANES * (i // 2):LANES * (i // 2 + 1)] = (
                pair.T.astype(BF16))

    m = None
    l_prev = None
    for u in range(-1, nunit + 1):
        do_scores = u + 1 < nunit
        do_softmax = 0 <= u < nunit
        do_pv = 1 <= u
        q_next = q_unit(u + 1) if do_scores else None
        mparts, lparts, ot = [], [], None
        for j, r0, n, s0 in chunks:
            if do_scores:
                st = lax.dot_general(k_refs[j][0, r0:r0 + n], q_next, (((1,), (1,)), ((), ())),
                                     preferred_element_type=F32)
                s_ref[(u + 1) % nslot, s0:s0 + n] = st
                mparts.append(jnp.max(st, axis=0, keepdims=True))
            if do_softmax:
                p = jnp.exp2(s_ref[u % nslot, s0:s0 + n] - m)
                lparts.append(jnp.sum(p, axis=0, keepdims=True))
                p_ref[u % nslot, s0:s0 + n] = p.astype(BF16)
            if do_pv:
                d = jnp.dot(vt_refs[j][0, :, r0:r0 + n], p_ref[(u - 1) % nslot, s0:s0 + n],
                            preferred_element_type=F32)
                ot = d if ot is None else ot + d
        if do_pv:
            emit(u - 1, ot, l_prev)
        if do_softmax:
            l_prev = functools.reduce(jnp.add, lparts)
        if do_scores:
            m = functools.reduce(jnp.maximum, mparts)


def _attn_bounded_kernel(*refs, n_kv):
    q_ref = refs[0]
    k_refs = refs[1:1 + n_kv]
    vt_refs = refs[1 + n_kv:1 + 2 * n_kv]
    shift_ref, o_ref, p_ref = refs[1 + 2 * n_kv:]
    group = N_HEADS // N_KV_HEADS
    nslot = p_ref.shape[0]
    sub = p_ref.shape[2]
    nunit = (q_ref.shape[1] // sub) * N_HEADS
    shift = shift_ref[0]

    chunks = []
    base = 0
    for j, kr in enumerate(k_refs):
        n = kr.shape[1]
        step = min(n, ATTN_KEY_CHUNK)
        chunks += [(j, r0, step, base + r0) for r0 in range(0, n, step)]
        base += n

    def q_unit(u):
        r, i = divmod(u, N_HEADS)
        return q_ref[0, sub * r:sub * (r + 1), LANES * i:LANES * (i + 1)]

    def v_ext(kvh, j, r0, n):
        vt = vt_refs[j][0, HEAD_DIM * kvh:HEAD_DIM * (kvh + 1), r0:r0 + n]
        return jnp.concatenate([vt, jnp.ones((2 * SUBLANES, n), BF16)], axis=0)

    pending = []

    def emit(u, ot):
        r, i = divmod(u, N_HEADS)
        pending.append(ot[:HEAD_DIM] * (1.0 / ot[HEAD_DIM:HEAD_DIM + 1]))
        if i % 2 == 1:
            pair = jnp.concatenate(pending, axis=0)
            pending.clear()
            o_ref[0, sub * r:sub * (r + 1), LANES * (i // 2):LANES * (i // 2 + 1)] = (
                pair.T.astype(BF16))

    for u in range(nunit + 1):
        q_u = q_unit(u) if u < nunit else None
        kvh_prev = ((u - 1) % N_HEADS) // group
        ot = None
        for j, r0, n, s0 in chunks:
            if u < nunit:
                st = lax.dot_general(k_refs[j][0, r0:r0 + n], q_u, (((1,), (1,)), ((), ())),
                                     preferred_element_type=F32)
                p_ref[u % nslot, s0:s0 + n] = jnp.exp2(st - shift).astype(BF16)
            if u >= 1:
                d = jnp.dot(v_ext(kvh_prev, j, r0, n), p_ref[(u - 1) % nslot, s0:s0 + n],
                            preferred_element_type=F32)
                ot = d if ot is None else ot + d
        if u >= 1:
            emit(u - 1, ot)


def _attn_call(q, ks, vts, tq, score_bound=None):
    b, s, _ = q.shape
    n_kv = len(ks)
    nk = sum(k.shape[1] for k in ks)
    sub = min(tq, ATTN_SUB)
    kv_map = lambda b_, t_: (b_, 0, 0)
    in_specs = ([pl.BlockSpec((1, tq, N_HEADS * LANES), lambda b_, t_: (b_, t_, 0))]
                + [pl.BlockSpec((1, k.shape[1], LANES), kv_map) for k in ks]
                + [pl.BlockSpec((1, LANES, vt.shape[2]), kv_map) for vt in vts])
    common = dict(
        grid=(b, s // tq),
        out_specs=pl.BlockSpec((1, tq, N_HEADS * HEAD_DIM), lambda b_, t_: (b_, t_, 0)),
        out_shape=jax.ShapeDtypeStruct((b, s, N_HEADS * HEAD_DIM), BF16),
        compiler_params=_cparams(2),
    )

    def exact(_):
        return pl.pallas_call(
            functools.partial(_attn_kernel, n_kv=n_kv),
            in_specs=in_specs,
            scratch_shapes=[pltpu.VMEM((ATTN_SLOTS, nk, sub), F32),
                            pltpu.VMEM((ATTN_SLOTS, nk, sub), BF16)],
            name="attn", **common,
        )(q, *ks, *vts)

    if score_bound is None:
        return exact(None)

    def bounded(shift):
        return pl.pallas_call(
            functools.partial(_attn_bounded_kernel, n_kv=n_kv),
            in_specs=in_specs + [pl.BlockSpec(memory_space=pltpu.SMEM)],
            scratch_shapes=[pltpu.VMEM((ATTN_SLOTS, nk, sub), BF16)],
            name="attn_bounded", **common,
        )(q, *ks, *vts, shift.reshape(1))

    return lax.cond(2.0 * score_bound <= ATTN_MAX_SHIFTED_RANGE, bounded, exact, score_bound)


def _mix_kernel(att_ref, u_ref, sv_ref, xp_ref, xpp_ref, xpn_ref, x_ref, g1_ref,
                ws_ref, bias_ref, wp_ref, ps_ref, icnt_ref, wo_ref, o_ref):
    t = x_ref.shape[1]
    ti = pl.program_id(1)
    nt = pl.num_programs(1)
    lane = lax.broadcasted_iota(jnp.int32, (CHUNK, LANES), 1)
    lo = lane < 64

    sv = sv_ref[0]
    rows = []
    for n in range(t // CHUNK):
        cols = []
        for jp in range(N_SGU_GROUPS // 2):
            blk = sv[CHUNK * n:CHUNK * (n + 1), LANES * jp:LANES * (jp + 1)]
            r0 = jnp.dot(ws_ref[2 * jp], blk, preferred_element_type=F32)
            r1 = jnp.dot(ws_ref[2 * jp + 1], blk, preferred_element_type=F32)
            cols.append(jnp.where(lo, r0, r1))
        rows.append(jnp.concatenate(cols, axis=1) + bias_ref[...])
    sgu = u_ref[0] * jnp.concatenate(rows, axis=0)

    xp = xp_ref[0]
    prev = jnp.where(ti > 0, xpp_ref[0], 0.0)
    nxt = jnp.where(ti < nt - 1, xpn_ref[0], 0.0)
    xe = jnp.concatenate([prev, xp, nxt], axis=0)
    lane_t = lax.broadcasted_iota(jnp.int32, (t, LANES), 1)
    lo_t = lane_t < 64

    def shifted(col, k_):
        return xe[HALO + k_:HALO + k_ + t, LANES * col:LANES * (col + 1)]

    def window(col, w, inner=None, inner_w=0):
        left = w // 2
        right = w - 1 - left
        ileft = inner_w // 2
        iright = inner_w - 1 - ileft
        acc = inner
        for k_ in range(-left, right + 1):
            if inner is not None and -ileft <= k_ <= iright:
                continue
            term = shifted(col, k_)
            acc = term if acc is None else acc + term
        return acc

    w2 = window(0, 2)
    w4 = window(0, 4, w2, 2)
    w8 = window(1, 8)
    w16 = window(1, 16, w8, 8)
    tot = jnp.concatenate([jnp.where(lo_t, w2, w4), jnp.where(lo_t, w8, w16)], axis=1)
    y = tot * icnt_ref[...] - xp
    pool = jnp.dot(y.astype(BF16), wp_ref[...], preferred_element_type=F32) * ps_ref[...]

    na = att_ref.shape[2]
    ns = sgu.shape[1]
    mix = jnp.dot(att_ref[0], wo_ref[0:na, :], preferred_element_type=F32)
    mix = mix + jnp.dot(sgu.astype(BF16), wo_ref[na:na + ns, :], preferred_element_type=F32)
    mix = mix + jnp.dot(pool.astype(BF16), wo_ref[na + ns:, :], preferred_element_type=F32)
    o_ref[0] = x_ref[0] + g1_ref[0] * mix


def _mix_call(att, u, sv, xp, x, g1, ws, bias, wp_bd, pscale, icnt, w_out, tile, cond_row=None):
    b, l, d = x.shape
    nt = l // tile
    hb = tile // HALO
    nhb = l // HALO
    crow = _cond_index(cond_row)
    row = lambda b_, t_: (b_, t_, 0)
    vec = lambda b_, t_: (crow(b_), 0, 0)
    prev = lambda b_, t_: (b_, jnp.maximum(t_ * hb - 1, 0), 0)
    nxt = lambda b_, t_: (b_, jnp.minimum((t_ + 1) * hb, nhb - 1), 0)
    return pl.pallas_call(
        _mix_kernel,
        grid=(b, nt),
        in_specs=[
            pl.BlockSpec((1, tile, att.shape[2]), row),
            pl.BlockSpec((1, tile, u.shape[2]), row),
            pl.BlockSpec((1, tile, sv.shape[2]), row),
            pl.BlockSpec((1, tile, xp.shape[2]), row),
            pl.BlockSpec((1, HALO, xp.shape[2]), prev),
            pl.BlockSpec((1, HALO, xp.shape[2]), nxt),
            pl.BlockSpec((1, tile, d), row),
            pl.BlockSpec((1, 1, d), vec),
            _const_spec(ws.shape),
            _const_spec(bias.shape),
            _const_spec(wp_bd.shape),
            _const_spec(pscale.shape),
            pl.BlockSpec((tile, icnt.shape[1]), lambda b_, t_: (t_, 0)),
            _const_spec(w_out.shape),
        ],
        out_specs=pl.BlockSpec((1, tile, d), row),
        out_shape=jax.ShapeDtypeStruct((b, l, d), F32),
        compiler_params=_cparams(2),
        name="mix",
    )(att, u, sv, xp, xp, xp, x, g1, ws, bias, wp_bd, pscale, icnt, w_out)


def _ffn_kernel(x_ref, xp_ref, xn_ref, sh_ref, sc_ref, g2_ref, ng_ref, wup_ref, cw_ref, cb_ref,
                wd_ref, fg_ref, o_ref, z_ref, *, final):
    t = x_ref.shape[1]
    ti = pl.program_id(1)
    nt = pl.num_programs(1)
    x = x_ref[0]
    ng = ng_ref[...]
    sh = sh_ref[0]
    sc = sc_ref[0]
    hprev = jnp.where(ti > 0, _rms_mod(xp_ref[0], ng, sh, sc), 0.0)
    hnext = jnp.where(ti < nt - 1, _rms_mod(xn_ref[0], ng, sh, sc), 0.0)
    h = jnp.concatenate([hprev, _rms_mod(x, ng, sh, sc), hnext], axis=0).astype(BF16)
    nch, fc, _ = wd_ref.shape
    d_ff = nch * fc

    nl = fc // LANES
    nslot = z_ref.shape[0]

    def up(c):
        slot = c % nslot
        zg = jnp.dot(h, wup_ref[:, fc * c:fc * (c + 1)], preferred_element_type=F32)
        zv = jnp.dot(h, wup_ref[:, d_ff + fc * c:d_ff + fc * (c + 1)], preferred_element_type=F32)
        for j in range(nl):
            z_ref[slot, j] = zg[:, LANES * j:LANES * (j + 1)]
            z_ref[slot, nl + j] = zv[:, LANES * j:LANES * (j + 1)]

    def conv(slot, j, cw, cb):
        return (z_ref[slot, j, HALO - 1:HALO - 1 + t] * cw[0:1]
                + z_ref[slot, j, HALO:HALO + t] * cw[1:2]
                + z_ref[slot, j, HALO + 1:HALO + 1 + t] * cw[2:3] + cb)

    acc = jnp.zeros((t, x.shape[1]), F32)
    for c in range(min(nslot - 1, nch)):
        up(c)
    for c in range(nch):
        if c + nslot - 1 < nch:
            up(c + nslot - 1)
        slot = c % nslot
        parts = []
        for j in range(nl):
            lg = slice(fc * c + LANES * j, fc * c + LANES * (j + 1))
            lv = slice(d_ff + fc * c + LANES * j, d_ff + fc * c + LANES * (j + 1))
            cg = conv(slot, j, cw_ref[:, lg], cb_ref[:, lg])
            cv = conv(slot, nl + j, cw_ref[:, lv], cb_ref[:, lv])
            parts.append((_silu(cg) * cv).astype(BF16))
        a = jnp.concatenate(parts, axis=1)
        acc = acc + jnp.dot(a, wd_ref[c], preferred_element_type=F32)
    y = x + g2_ref[0] * acc
    if final:
        ms = jnp.mean(y * y, axis=-1, keepdims=True)
        y = y * lax.rsqrt(ms + EPS) * fg_ref[...]
    o_ref[0] = y


def _ffn_call(x, sh, sc, g2, ng, wup, cw, cb, wd, fg, tile, final, cond_row=None):
    b, l, d = x.shape
    nt = l // tile
    hb = tile // HALO
    nhb = l // HALO
    crow = _cond_index(cond_row)
    row = lambda b_, t_: (b_, t_, 0)
    vec = lambda b_, t_: (crow(b_), 0, 0)
    prev = lambda b_, t_: (b_, jnp.maximum(t_ * hb - 1, 0), 0)
    nxt = lambda b_, t_: (b_, jnp.minimum((t_ + 1) * hb, nhb - 1), 0)
    return pl.pallas_call(
        functools.partial(_ffn_kernel, final=final),
        grid=(b, nt),
        in_specs=[
            pl.BlockSpec((1, tile, d), row),
            pl.BlockSpec((1, HALO, d), prev),
            pl.BlockSpec((1, HALO, d), nxt),
            pl.BlockSpec((1, 1, d), vec),
            pl.BlockSpec((1, 1, d), vec),
            pl.BlockSpec((1, 1, d), vec),
            _const_spec(ng.shape),
            _const_spec(wup.shape),
            _const_spec(cw.shape),
            _const_spec(cb.shape),
            _const_spec(wd.shape),
            _const_spec(fg.shape),
        ],
        out_specs=pl.BlockSpec((1, tile, d), row),
        out_shape=jax.ShapeDtypeStruct((b, l, d), F32),
        scratch_shapes=[pltpu.VMEM(
            (FFN_Z_SLOTS, 2 * wd.shape[1] // LANES, tile + 2 * HALO, LANES), F32)],
        compiler_params=_cparams(2),
        name="ffn_final" if final else "ffn",
    )(x, x, x, sh, sc, g2, ng, wup, cw, cb, wd, fg)


def _rope_tables(n):
    pos = jnp.arange(n)
    row = (pos // GRID_W).astype(F32)
    col = (pos % GRID_W).astype(F32)
    axis_dim = HEAD_DIM // 2
    inv = ROPE_BASE ** (-jnp.arange(0, axis_dim, 2, dtype=F32) / axis_dim)
    ang_r = row[:, None] * inv[None, :]
    ang_c = col[:, None] * inv[None, :]
    cr, sr, cc, sn = jnp.cos(ang_r), jnp.sin(ang_r), jnp.cos(ang_c), jnp.sin(ang_c)
    cos_h = jnp.concatenate([cr, cr, cc, cc], axis=1)
    sin_h = jnp.concatenate([-sr, sr, -sn, sn], axis=1)
    return jnp.tile(cos_h, (1, 2)), jnp.tile(sin_h, (1, 2))


def _pool_inv_count(l):
    t = np.arange(l)
    cols = []
    for w in POOL_WINDOWS:
        left = w // 2
        right = w - 1 - left
        cnt = (np.minimum(t + right, l - 1) - np.maximum(t - left, 0) + 1).astype(np.float32)
        cols.append(np.repeat((np.float32(1.0) / cnt)[:, None], POOL_GROUP_DIM, axis=1))
    return jnp.asarray(np.concatenate(cols, axis=1))


def _block_diag(w):
    g, a, b = w.shape
    out = jnp.zeros((g * a, g * b), w.dtype)
    for i in range(g):
        out = out.at[i * a:(i + 1) * a, i * b:(i + 1) * b].set(w[i])
    return out


def _head_mean_matrix(n_heads):
    n = n_heads * HEAD_DIM
    idx = np.arange(n) // HEAD_DIM
    return jnp.asarray((idx[:, None] == idx[None, :]).astype(np.float32) / HEAD_DIM, dtype=BF16)


def kernel(x, c, ctx, c_ctx, w_mod, b_mod, norm1_g, w_in, q_gain, k_gain, w_s, b_s, w_pool,
           pool_scale, w_out, norm2_g, w_up, conv_w, conv_b, w_down, final_g):
    b, s, d = x.shape
    cl = ctx.shape[1]
    depth = w_mod.shape[0]
    d_ff = w_down.shape[1]
    fc = FFN_CHUNK
    nch = d_ff // fc

    rows = ((b + 1 + SUBLANES - 1) // SUBLANES) * SUBLANES
    cvec = jnp.zeros((rows, d), F32).at[:b].set(c).at[b].set(c_ctx)
    mod = _mod_call(cvec, w_mod, b_mod)

    cos_x, sin_x = _rope_tables(s)
    cos_c = jnp.ones((cl, LANES), F32)
    sin_c = jnp.zeros((cl, LANES), F32)
    e_q = _head_mean_matrix(N_HEADS)
    e_k = _head_mean_matrix(N_KV_HEADS)
    icnt_x = _pool_inv_count(s)
    icnt_c = _pool_inv_count(cl)
    fg = final_g.reshape(1, d)

    tile_c = min(256, cl)
    big_x = min(512, s)
    wide_x = min(1024, s)
    proj_x = min(2048, s)

    for i in range(depth):
        last = i == depth - 1
        m6 = [mod[i, j].reshape(rows, 1, d) for j in range(6)]
        g1n = norm1_g[i].reshape(1, d)
        g2n = norm2_g[i].reshape(1, d)
        w_in_b = w_in[i].astype(BF16)
        w_out_b = w_out[i].astype(BF16)
        gain = jnp.concatenate([jnp.tile(q_gain[i] * (HEAD_DIM ** -0.5 * math.log2(math.e)), N_HEADS),
                                jnp.tile(k_gain[i], N_KV_HEADS)]).reshape(1, -1)
        ws_b = w_s[i].astype(BF16)
        bias = jnp.repeat(b_s[i].T, d // 4 // N_SGU_GROUPS, axis=1)
        wp_bd = _block_diag(w_pool[i]).astype(BF16)
        pscale = pool_scale[i].reshape(1, -1)
        wup = w_up[i].astype(BF16)
        cw = conv_w[i]
        cb = conv_b[i].reshape(1, -1)
        wd = w_down[i].astype(BF16).reshape(nch, fc, d)

        qx, kx, vtx, ux, svx, xpx = _proj_call(x, m6[0], m6[1], g1n, w_in_b, e_q, e_k, gain,
                                               cos_x, sin_x, proj_x)
        if last:
            kc, vtc = _proj_call(ctx, m6[0], m6[1], g1n, w_in_b, e_q, e_k, gain,
                                 cos_c, sin_c, tile_c, cond_row=b, kv_only=True)
        else:
            qc, kc, vtc, uc, svc, xpc = _proj_call(ctx, m6[0], m6[1], g1n, w_in_b, e_q, e_k, gain,
                                                   cos_c, sin_c, tile_c, cond_row=b)
        score_bound = (1.01 * HEAD_DIM * jnp.max(jnp.abs(gain[0, :N_HEADS * HEAD_DIM]))
                       * jnp.max(jnp.abs(k_gain[i])))
        att_x = _attn_call(qx, [kx, kc], [vtx, vtc], wide_x, score_bound)
        x = _mix_call(att_x, ux, svx, xpx, x, m6[2], ws_b, bias, wp_bd, pscale, icnt_x,
                      w_out_b, wide_x)
        x = _ffn_call(x, m6[3], m6[4], m6[5], g2n, wup, cw, cb, wd, fg,
                      big_x, last)
        if not last:
            att_c = _attn_call(qc, [kc], [vtc], tile_c)
            ctx = _mix_call(att_c, uc, svc, xpc, ctx, m6[2], ws_b, bias, wp_bd, pscale, icnt_c,
                            w_out_b, tile_c, cond_row=b)
            ctx = _ffn_call(ctx, m6[3], m6[4], m6[5], g2n, wup, cw, cb, wd, fg,
                            tile_c, False, cond_row=b)
    return x
```

```python
import functools
import math

import numpy as np
import jax
import jax.numpy as jnp
from jax import lax
from jax.experimental import pallas as pl
from jax.experimental.pallas import tpu as pltpu

F32 = jnp.float32
BF16 = jnp.bfloat16

EPS = 1e-6
GRID_W = 64
HEAD_DIM = 64
N_HEADS = 8
N_KV_HEADS = 2
ROPE_BASE = 10000.0
CHUNK = 128
N_SGU_GROUPS = 4
POOL_WINDOWS = (2, 4, 8, 16)
POOL_GROUP_DIM = 64
LANES = 128
SUBLANES = 8
HALO = SUBLANES
VMEM_LIMIT = 56 * 1024 * 1024
FFN_CHUNK = 256
FFN_Z_SLOTS = 4
ATTN_SLOTS = 2
ATTN_SUB = 256
ATTN_KEY_CHUNK = 1024
ATTN_MAX_SHIFTED_RANGE = 100.0
PROJ_SUB = 256


def _cparams(n_axes):
    return pltpu.CompilerParams(
        dimension_semantics=("arbitrary",) * n_axes, vmem_limit_bytes=VMEM_LIMIT)


def _const_spec(shape):
    nd = len(shape)
    return pl.BlockSpec(shape, lambda *_: (0,) * nd, pipeline_mode=pl.Buffered(1))


def _silu(x):
    return x * (1.0 / (1.0 + jnp.exp(-x)))


def _rms_mod(x, g, shift, scale):
    ms = jnp.mean(x * x, axis=-1, keepdims=True)
    y = x * lax.rsqrt(ms + EPS) * g
    return y * (1.0 + scale) + shift


def _mod_kernel(c_ref, w_ref, b_ref, o_ref):
    s = _silu(c_ref[...]).astype(BF16)
    o_ref[0, 0] = jnp.dot(s, w_ref[0].astype(BF16), preferred_element_type=F32) + b_ref[0]


def _mod_call(cvec, w_mod, b_mod):
    depth, d, n = w_mod.shape
    rows = cvec.shape[0]
    return pl.pallas_call(
        _mod_kernel,
        grid=(depth, n // d),
        in_specs=[
            pl.BlockSpec((rows, d), lambda i, j: (0, 0)),
            pl.BlockSpec((1, d, d), lambda i, j: (i, 0, j)),
            pl.BlockSpec((1, 1, d), lambda i, j: (i, 0, j)),
        ],
        out_specs=pl.BlockSpec((1, 1, rows, d), lambda i, j: (i, j, 0, 0)),
        out_shape=jax.ShapeDtypeStruct((depth, n // d, rows, d), F32),
        compiler_params=_cparams(2),
        name="mod",
    )(cvec, w_mod, b_mod.reshape(depth, 1, n))


def _proj_kernel(x_ref, sh_ref, sc_ref, g_ref, w_ref, eq_ref, ek_ref, gain_ref, cos_ref, sin_ref,
                 *out_refs, kv_only):
    if kv_only:
        k_ref, vt_ref = out_refs
    else:
        q_ref, k_ref, vt_ref, u_ref, sv_ref, xp_ref = out_refs
    t = x_ref.shape[1]
    sub = min(t, PROJ_SUB)
    nq = N_HEADS * HEAD_DIM
    nqk = (N_HEADS + N_KV_HEADS) * HEAD_DIM
    nv = N_KV_HEADS * HEAD_DIM
    col0 = nq if kv_only else 0
    lane = lax.broadcasted_iota(jnp.int32, (sub, LANES), 1)
    first = (lane % 32) < 16
    lo = lane < HEAD_DIM

    def project(r):
        rows = slice(sub * r, sub * (r + 1))
        h = _rms_mod(x_ref[0, rows], g_ref[...], sh_ref[0], sc_ref[0]).astype(BF16)
        w = w_ref[:, col0:nqk + nv] if kv_only else w_ref[...]
        return jnp.dot(h, w, preferred_element_type=F32)

    def finish(r, p):
        rows = slice(sub * r, sub * (r + 1))
        qk = p[:, :nqk - col0]
        sq = (qk * qk).astype(BF16)
        msk = jnp.dot(sq[:, nq - col0:], ek_ref[...], preferred_element_type=F32)
        if kv_only:
            msh = msk
        else:
            msh = jnp.concatenate(
                [jnp.dot(sq[:, :nq], eq_ref[...], preferred_element_type=F32), msk], axis=1)
        qkn = qk * lax.rsqrt(msh + EPS) * gain_ref[:, col0:]
        c = cos_ref[rows]
        s = sin_ref[rows]
        for j in range(col0 // LANES, nqk // LANES):
            blk = qkn[:, LANES * j - col0:LANES * (j + 1) - col0]
            sw = jnp.where(first, pltpu.roll(blk, LANES - 16, axis=1), pltpu.roll(blk, 16, axis=1))
            rot = blk * c + sw * s
            if j < N_HEADS // 2:
                kvh = (2 * j) // (N_HEADS // N_KV_HEADS)
                rr = pltpu.roll(rot, HEAD_DIM, axis=1)
                keep = lo if kvh == 0 else jnp.logical_not(lo)
                even = rot if kvh == 0 else rr
                odd = rr if kvh == 0 else rot
                q_ref[0, rows, LANES * (2 * j):LANES * (2 * j + 1)] = (
                    jnp.where(keep, even, 0.0).astype(BF16))
                q_ref[0, rows, LANES * (2 * j + 1):LANES * (2 * j + 2)] = (
                    jnp.where(keep, odd, 0.0).astype(BF16))
            else:
                k_ref[0, rows] = rot.astype(BF16)
        vt_ref[0, :, rows] = p[:, nqk - col0:nqk - col0 + nv].T.astype(BF16)
        if not kv_only:
            u_ref[0, rows] = p[:, 768:1024]
            sv_ref[0, rows] = p[:, 1024:1280].astype(BF16)
            xp_ref[0, rows] = p[:, 1280:1536]

    nsub = t // sub
    p_next = project(0)
    for r in range(nsub):
        p = p_next
        if r + 1 < nsub:
            p_next = project(r + 1)
        finish(r, p)


def _cond_index(cond_row):
    return (lambda b_: b_) if cond_row is None else (lambda b_: cond_row)


def _proj_call(x, sh, sc, g, w_in, e_q, e_k, gain, cos_t, sin_t, tile, cond_row=None,
               kv_only=False):
    b, l, d = x.shape
    nt = l // tile
    n_in = w_in.shape[1]
    crow = _cond_index(cond_row)
    row = lambda t_, b_: (b_, t_, 0)
    vec = lambda t_, b_: (crow(b_), 0, 0)
    tab = lambda t_, b_: (t_, 0)
    out_specs = [
        pl.BlockSpec((1, tile, N_HEADS * LANES), row),
        pl.BlockSpec((1, tile, LANES), row),
        pl.BlockSpec((1, LANES, tile), lambda t_, b_: (b_, 0, t_)),
        pl.BlockSpec((1, tile, 256), row),
        pl.BlockSpec((1, tile, 256), row),
        pl.BlockSpec((1, tile, 256), row),
    ]
    out_shape = [
        jax.ShapeDtypeStruct((b, l, N_HEADS * LANES), BF16),
        jax.ShapeDtypeStruct((b, l, LANES), BF16),
        jax.ShapeDtypeStruct((b, LANES, l), BF16),
        jax.ShapeDtypeStruct((b, l, 256), F32),
        jax.ShapeDtypeStruct((b, l, 256), BF16),
        jax.ShapeDtypeStruct((b, l, 256), F32),
    ]
    if kv_only:
        out_specs, out_shape = out_specs[1:3], out_shape[1:3]
    return pl.pallas_call(
        functools.partial(_proj_kernel, kv_only=kv_only),
        grid=(nt, b),
        in_specs=[
            pl.BlockSpec((1, tile, d), row),
            pl.BlockSpec((1, 1, d), vec),
            pl.BlockSpec((1, 1, d), vec),
            _const_spec((1, d)),
            _const_spec((d, n_in)),
            _const_spec(e_q.shape),
            _const_spec(e_k.shape),
            _const_spec(gain.shape),
            pl.BlockSpec((tile, LANES), tab),
            pl.BlockSpec((tile, LANES), tab),
        ],
        out_specs=out_specs,
        out_shape=out_shape,
        compiler_params=_cparams(2),
        name="proj_kv" if kv_only else "proj",
    )(x, sh, sc, g, w_in, e_q, e_k, gain, cos_t, sin_t)


def _attn_kernel(*refs, n_kv):
    q_ref = refs[0]
    k_refs = refs[1:1 + n_kv]
    vt_refs = refs[1 + n_kv:1 + 2 * n_kv]
    o_ref, s_ref, p_ref = refs[1 + 2 * n_kv:]
    group = N_HEADS // N_KV_HEADS
    nslot = s_ref.shape[0]
    sub = s_ref.shape[2]
    nunit = (q_ref.shape[1] // sub) * N_HEADS

    chunks = []
    base = 0
    for j, kr in enumerate(k_refs):
        n = kr.shape[1]
        step = min(n, ATTN_KEY_CHUNK)
        chunks += [(j, r0, step, base + r0) for r0 in range(0, n, step)]
        base += n

    def q_unit(u):
        r, i = divmod(u, N_HEADS)
        return q_ref[0, sub * r:sub * (r + 1), LANES * i:LANES * (i + 1)]

    pending = []

    def emit(u, ot, l):
        r, i = divmod(u, N_HEADS)
        kvh = i // group
        pending.append(ot[HEAD_DIM * kvh:HEAD_DIM * (kvh + 1)] * (1.0 / l))
        if i % 2 == 1:
            pair = jnp.concatenate(pending, axis=0)
            pending.clear()
            o_ref[0, sub * r:sub * (r + 1), LANES * (i // 2):LANES * (i // 2 + 1)] = (
                pair.T.astype(BF16))

    m = None
    l_prev = None
    for u in range(-1, nunit + 1):
        do_scores = u + 1 < nunit
        do_softmax = 0 <= u < nunit
        do_pv = 1 <= u
        q_next = q_unit(u + 1) if do_scores else None
        mparts, lparts, ot = [], [], None
        for j, r0, n, s0 in chunks:
            if do_scores:
                st = lax.dot_general(k_refs[j][0, r0:r0 + n], q_next, (((1,), (1,)), ((), ())),
                                     preferred_element_type=F32)
                s_ref[(u + 1) % nslot, s0:s0 + n] = st
                mparts.append(jnp.max(st, axis=0, keepdims=True))
            if do_softmax:
                p = jnp.exp2(s_ref[u % nslot, s0:s0 + n] - m)
                lparts.append(jnp.sum(p, axis=0, keepdims=True))
                p_ref[u % nslot, s0:s0 + n] = p.astype(BF16)
            if do_pv:
                d = jnp.dot(vt_refs[j][0, :, r0:r0 + n], p_ref[(u - 1) % nslot, s0:s0 + n],
                            preferred_element_type=F32)
                ot = d if ot is None else ot + d
        if do_pv:
            emit(u - 1, ot, l_prev)
        if do_softmax:
            l_prev = functools.reduce(jnp.add, lparts)
        if do_scores:
            m = functools.reduce(jnp.maximum, mparts)


def _attn_bounded_kernel(*refs, n_kv):
    q_ref = refs[0]
    k_refs = refs[1:1 + n_kv]
    vt_refs = refs[1 + n_kv:1 + 2 * n_kv]
    shift_ref, o_ref, p_ref = refs[1 + 2 * n_kv:]
    group = N_HEADS // N_KV_HEADS
    nslot = p_ref.shape[0]
    sub = p_ref.shape[2]
    nunit = (q_ref.shape[1] // sub) * N_HEADS
    shift = shift_ref[0]

    chunks = []
    base = 0
    for j, kr in enumerate(k_refs):
        n = kr.shape[1]
        step = min(n, ATTN_KEY_CHUNK)
        chunks += [(j, r0, step, base + r0) for r0 in range(0, n, step)]
        base += n

    def q_unit(u):
        r, i = divmod(u, N_HEADS)
        return q_ref[0, sub * r:sub * (r + 1), LANES * i:LANES * (i + 1)]

    def v_ext(kvh, j, r0, n):
        vt = vt_refs[j][0, HEAD_DIM * kvh:HEAD_DIM * (kvh + 1), r0:r0 + n]
        return jnp.concatenate([vt, jnp.ones((2 * SUBLANES, n), BF16)], axis=0)

    pending = []

    def emit(u, ot):
        r, i = divmod(u, N_HEADS)
        pending.append(ot[:HEAD_DIM] * (1.0 / ot[HEAD_DIM:HEAD_DIM + 1]))
        if i % 2 == 1:
            pair = jnp.concatenate(pending, axis=0)
            pending.clear()
            o_ref[0, sub * r:sub * (r + 1), LANES * (i // 2):LANES * (i // 2 + 1)] = (
                pair.T.astype(BF16))

    for u in range(nunit + 1):
        q_u = q_unit(u) if u < nunit else None
        kvh_prev = ((u - 1) % N_HEADS) // group
        ot = None
        for j, r0, n, s0 in chunks:
            if u < nunit:
                st = lax.dot_general(k_refs[j][0, r0:r0 + n], q_u, (((1,), (1,)), ((), ())),
                                     preferred_element_type=F32)
                p_ref[u % nslot, s0:s0 + n] = jnp.exp2(st - shift).astype(BF16)
            if u >= 1:
                d = jnp.dot(v_ext(kvh_prev, j, r0, n), p_ref[(u - 1) % nslot, s0:s0 + n],
                            preferred_element_type=F32)
                ot = d if ot is None else ot + d
        if u >= 1:
            emit(u - 1, ot)


def _attn_call(q, ks, vts, tq, score_bound=None):
    b, s, _ = q.shape
    n_kv = len(ks)
    nk = sum(k.shape[1] for k in ks)
    sub = min(tq, ATTN_SUB)
    kv_map = lambda b_, t_: (b_, 0, 0)
    in_specs = ([pl.BlockSpec((1, tq, N_HEADS * LANES), lambda b_, t_: (b_, t_, 0))]
                + [pl.BlockSpec((1, k.shape[1], LANES), kv_map) for k in ks]
                + [pl.BlockSpec((1, LANES, vt.shape[2]), kv_map) for vt in vts])
    common = dict(
        grid=(b, s // tq),
        out_specs=pl.BlockSpec((1, tq, N_HEADS * HEAD_DIM), lambda b_, t_: (b_, t_, 0)),
        out_shape=jax.ShapeDtypeStruct((b, s, N_HEADS * HEAD_DIM), BF16),
        compiler_params=_cparams(2),
    )

    def exact(_):
        return pl.pallas_call(
            functools.partial(_attn_kernel, n_kv=n_kv),
            in_specs=in_specs,
            scratch_shapes=[pltpu.VMEM((ATTN_SLOTS, nk, sub), F32),
                            pltpu.VMEM((ATTN_SLOTS, nk, sub), BF16)],
            name="attn", **common,
        )(q, *ks, *vts)

    if score_bound is None:
        return exact(None)

    def bounded(shift):
        return pl.pallas_call(
            functools.partial(_attn_bounded_kernel, n_kv=n_kv),
            in_specs=in_specs + [pl.BlockSpec(memory_space=pltpu.SMEM)],
            scratch_shapes=[pltpu.VMEM((ATTN_SLOTS, nk, sub), BF16)],
            name="attn_bounded", **common,
        )(q, *ks, *vts, shift.reshape(1))

    return lax.cond(2.0 * score_bound <= ATTN_MAX_SHIFTED_RANGE, bounded, exact, score_bound)


def _mix_kernel(att_ref, u_ref, sv_ref, xp_ref, xpp_ref, xpn_ref, x_ref, g1_ref,
                ws_ref, bias_ref, wp_ref, ps_ref, icnt_ref, wo_ref, o_ref):
    t = x_ref.shape[1]
    ti = pl.program_id(1)
    nt = pl.num_programs(1)
    lane = lax.broadcasted_iota(jnp.int32, (CHUNK, LANES), 1)
    lo = lane < 64

    sv = sv_ref[0]
    rows = []
    for n in range(t // CHUNK):
        cols = []
        for jp in range(N_SGU_GROUPS // 2):
            blk = sv[CHUNK * n:CHUNK * (n + 1), LANES * jp:LANES * (jp + 1)]
            r0 = jnp.dot(ws_ref[2 * jp], blk, preferred_element_type=F32)
            r1 = jnp.dot(ws_ref[2 * jp + 1], blk, preferred_element_type=F32)
            cols.append(jnp.where(lo, r0, r1))
        rows.append(jnp.concatenate(cols, axis=1) + bias_ref[...])
    sgu = u_ref[0] * jnp.concatenate(rows, axis=0)

    xp = xp_ref[0]
    prev = jnp.where(ti > 0, xpp_ref[0], 0.0)
    nxt = jnp.where(ti < nt - 1, xpn_ref[0], 0.0)
    xe = jnp.concatenate([prev, xp, nxt], axis=0)
    lane_t = lax.broadcasted_iota(jnp.int32, (t, LANES), 1)
    lo_t = lane_t < 64

    def shifted(col, k_):
        return xe[HALO + k_:HALO + k_ + t, LANES * col:LANES * (col + 1)]

    def window(col, w, inner=None, inner_w=0):
        left = w // 2
        right = w - 1 - left
        ileft = inner_w // 2
        iright = inner_w - 1 - ileft
        acc = inner
        for k_ in range(-left, right + 1):
            if inner is not None and -ileft <= k_ <= iright:
                continue
            term = shifted(col, k_)
            acc = term if acc is None else acc + term
        return acc

    w2 = window(0, 2)
    w4 = window(0, 4, w2, 2)
    w8 = window(1, 8)
    w16 = window(1, 16, w8, 8)
    tot = jnp.concatenate([jnp.where(lo_t, w2, w4), jnp.where(lo_t, w8, w16)], axis=1)
    y = tot * icnt_ref[...] - xp
    pool = jnp.dot(y.astype(BF16), wp_ref[...], preferred_element_type=F32) * ps_ref[...]

    na = att_ref.shape[2]
    ns = sgu.shape[1]
    mix = jnp.dot(att_ref[0], wo_ref[0:na, :], preferred_element_type=F32)
    mix = mix + jnp.dot(sgu.astype(BF16), wo_ref[na:na + ns, :], preferred_element_type=F32)
    mix = mix + jnp.dot(pool.astype(BF16), wo_ref[na + ns:, :], preferred_element_type=F32)
    o_ref[0] = x_ref[0] + g1_ref[0] * mix


def _mix_call(att, u, sv, xp, x, g1, ws, bias, wp_bd, pscale, icnt, w_out, tile, cond_row=None):
    b, l, d = x.shape
    nt = l // tile
    hb = tile // HALO
    nhb = l // HALO
    crow = _cond_index(cond_row)
    row = lambda b_, t_: (b_, t_, 0)
    vec = lambda b_, t_: (crow(b_), 0, 0)
    prev = lambda b_, t_: (b_, jnp.maximum(t_ * hb - 1, 0), 0)
    nxt = lambda b_, t_: (b_, jnp.minimum((t_ + 1) * hb, nhb - 1), 0)
    return pl.pallas_call(
        _mix_kernel,
        grid=(b, nt),
        in_specs=[
            pl.BlockSpec((1, tile, att.shape[2]), row),
            pl.BlockSpec((1, tile, 256), row),
            pl.BlockSpec((1, tile, 256), row),
            pl.BlockSpec((1, tile, 256), row),
            pl.BlockSpec((1, HALO, 256), prev),
            pl.BlockSpec((1, HALO, 256), nxt),
            pl.BlockSpec((1, tile, d), row),
            pl.BlockSpec((1, 1, d), vec),
            _const_spec(ws.shape),
            _const_spec(bias.shape),
            _const_spec(wp_bd.shape),
            _const_spec(pscale.shape),
            pl.BlockSpec((tile, 256), lambda b_, t_: (t_, 0)),
            _const_spec(w_out.shape),
        ],
        out_specs=pl.BlockSpec((1, tile, d), row),
        out_shape=jax.ShapeDtypeStruct((b, l, d), F32),
        compiler_params=_cparams(2),
        name="mix",
    )(att, u, sv, xp, xp, xp, x, g1, ws, bias, wp_bd, pscale, icnt, w_out)


def _ffn_kernel(x_ref, xp_ref, xn_ref, sh_ref, sc_ref, g2_ref, ng_ref, wup_ref, cw_ref, cb_ref,
                wd_ref, fg_ref, o_ref, z_ref, *, final):
    t = x_ref.shape[1]
    ti = pl.program_id(1)
    nt = pl.num_programs(1)
    x = x_ref[0]
    ng = ng_ref[...]
    sh = sh_ref[0]
    sc = sc_ref[0]
    hprev = jnp.where(ti > 0, _rms_mod(xp_ref[0], ng, sh, sc), 0.0)
    hnext = jnp.where(ti < nt - 1, _rms_mod(xn_ref[0], ng, sh, sc), 0.0)
    h = jnp.concatenate([hprev, _rms_mod(x, ng, sh, sc), hnext], axis=0).astype(BF16)
    nch, fc, _ = wd_ref.shape
    d_ff = nch * fc

    nl = fc // LANES
    nslot = z_ref.shape[0]

    def up(c):
        slot = c % nslot
        zg = jnp.dot(h, wup_ref[:, fc * c:fc * (c + 1)], preferred_element_type=F32)
        zv = jnp.dot(h, wup_ref[:, d_ff + fc * c:d_ff + fc * (c + 1)], preferred_element_type=F32)
        for j in range(nl):
            z_ref[slot, j] = zg[:, LANES * j:LANES * (j + 1)]
            z_ref[slot, nl + j] = zv[:, LANES * j:LANES * (j + 1)]

    def conv(slot, j, cw, cb):
        return (z_ref[slot, j, HALO - 1:HALO - 1 + t] * cw[0:1]
                + z_ref[slot, j, HALO:HALO + t] * cw[1:2]
                + z_ref[slot, j, HALO + 1:HALO + 1 + t] * cw[2:3] + cb)

    acc = jnp.zeros((t, x.shape[1]), F32)
    for c in range(min(nslot - 1, nch)):
        up(c)
    for c in range(nch):
        if c + nslot - 1 < nch:
            up(c + nslot - 1)
        slot = c % nslot
        parts = []
        for j in range(nl):
            lg = slice(fc * c + LANES * j, fc * c + LANES * (j + 1))
            lv = slice(d_ff + fc * c + LANES * j, d_ff + fc * c + LANES * (j + 1))
            cg = conv(slot, j, cw_ref[:, lg], cb_ref[:, lg])
            cv = conv(slot, nl + j, cw_ref[:, lv], cb_ref[:, lv])
            parts.append((_silu(cg) * cv).astype(BF16))
        a = jnp.concatenate(parts, axis=1)
        acc = acc + jnp.dot(a, wd_ref[c], preferred_element_type=F32)
    y = x + g2_ref[0] * acc
    if final:
        ms = jnp.mean(y * y, axis=-1, keepdims=True)
        y = y * lax.rsqrt(ms + EPS) * fg_ref[...]
    o_ref[0] = y


def _ffn_call(x, sh, sc, g2, ng, wup, cw, cb, wd, fg, tile, final, cond_row=None):
    b, l, d = x.shape
    nt = l // tile
    hb = tile // HALO
    nhb = l // HALO
    crow = _cond_index(cond_row)
    row = lambda b_, t_: (b_, t_, 0)
    vec = lambda b_, t_: (crow(b_), 0, 0)
    prev = lambda b_, t_: (b_, jnp.maximum(t_ * hb - 1, 0), 0)
    nxt = lambda b_, t_: (b_, jnp.minimum((t_ + 1) * hb, nhb - 1), 0)
    return pl.pallas_call(
        functools.partial(_ffn_kernel, final=final),
        grid=(b, nt),
        in_specs=[
            pl.BlockSpec((1, tile, d), row),
            pl.BlockSpec((1, HALO, d), prev),
            pl.BlockSpec((1, HALO, d), nxt),
            pl.BlockSpec((1, 1, d), vec),
            pl.BlockSpec((1, 1, d), vec),
            pl.BlockSpec((1, 1, d), vec),
            _const_spec(ng.shape),
            _const_spec(wup.shape),
            _const_spec(cw.shape),
            _const_spec(cb.shape),
            _const_spec(wd.shape),
            _const_spec(fg.shape),
        ],
        out_specs=pl.BlockSpec((1, tile, d), row),
        out_shape=jax.ShapeDtypeStruct((b, l, d), F32),
        scratch_shapes=[pltpu.VMEM(
            (FFN_Z_SLOTS, 2 * wd.shape[1] // LANES, tile + 2 * HALO, LANES), F32)],
        compiler_params=_cparams(2),
        name="ffn_final" if final else "ffn",
    )(x, x, x, sh, sc, g2, ng, wup, cw, cb, wd, fg)


def _rope_tables(n):
    pos = jnp.arange(n)
    row = (pos // GRID_W).astype(F32)
    col = (pos % GRID_W).astype(F32)
    axis_dim = HEAD_DIM // 2
    inv = ROPE_BASE ** (-jnp.arange(0, axis_dim, 2, dtype=F32) / axis_dim)
    ang_r = row[:, None] * inv[None, :]
    ang_c = col[:, None] * inv[None, :]
    cr, sr, cc, sn = jnp.cos(ang_r), jnp.sin(ang_r), jnp.cos(ang_c), jnp.sin(ang_c)
    cos_h = jnp.concatenate([cr, cr, cc, cc], axis=1)
    sin_h = jnp.concatenate([-sr, sr, -sn, sn], axis=1)
    return jnp.tile(cos_h, (1, 2)), jnp.tile(sin_h, (1, 2))


def _pool_inv_count(l):
    t = np.arange(l)
    cols = []
    for w in POOL_WINDOWS:
        left = w // 2
        right = w - 1 - left
        cnt = (np.minimum(t + right, l - 1) - np.maximum(t - left, 0) + 1).astype(np.float32)
        cols.append(np.repeat((np.float32(1.0) / cnt)[:, None], POOL_GROUP_DIM, axis=1))
    return jnp.asarray(np.concatenate(cols, axis=1))


def _block_diag(w):
    g, a, b = w.shape
    out = jnp.zeros((g * a, g * b), w.dtype)
    for i in range(g):
        out = out.at[i * a:(i + 1) * a, i * b:(i + 1) * b].set(w[i])
    return out


def _head_mean_matrix(n_heads):
    n = n_heads * HEAD_DIM
    idx = np.arange(n) // HEAD_DIM
    return jnp.asarray((idx[:, None] == idx[None, :]).astype(np.float32) / HEAD_DIM, dtype=BF16)


def kernel(x, c, ctx, c_ctx, w_mod, b_mod, norm1_g, w_in, q_gain, k_gain, w_s, b_s, w_pool,
           pool_scale, w_out, norm2_g, w_up, conv_w, conv_b, w_down, final_g):
    b, s, d = x.shape
    cl = ctx.shape[1]
    depth = w_mod.shape[0]
    d_ff = w_down.shape[1]
    fc = FFN_CHUNK
    nch = d_ff // fc

    rows = ((b + 1 + SUBLANES - 1) // SUBLANES) * SUBLANES
    cvec = jnp.zeros((rows, d), F32).at[:b].set(c).at[b].set(c_ctx)
    mod = _mod_call(cvec, w_mod, b_mod)

    cos_x, sin_x = _rope_tables(s)
    cos_c = jnp.ones((cl, LANES), F32)
    sin_c = jnp.zeros((cl, LANES), F32)
    e_q = _head_mean_matrix(N_HEADS)
    e_k = _head_mean_matrix(N_KV_HEADS)
    icnt_x = _pool_inv_count(s)
    icnt_c = _pool_inv_count(cl)
    fg = final_g.reshape(1, d)

    tile_c = min(256, cl)
    big_x = min(512, s)
    wide_x = min(1024, s)
    proj_x = min(2048, s)

    for i in range(depth):
        last = i == depth - 1
        m6 = [mod[i, j].reshape(rows, 1, d) for j in range(6)]
        g1n = norm1_g[i].reshape(1, d)
        g2n = norm2_g[i].reshape(1, d)
        w_in_b = w_in[i].astype(BF16)
        w_out_b = w_out[i].astype(BF16)
        gain = jnp.concatenate([jnp.tile(q_gain[i] * (HEAD_DIM ** -0.5 * math.log2(math.e)), N_HEADS),
                                jnp.tile(k_gain[i], N_KV_HEADS)]).reshape(1, -1)
        ws_b = w_s[i].astype(BF16)
        bias = jnp.repeat(b_s[i].T, d // 4 // N_SGU_GROUPS, axis=1)
        wp_bd = _block_diag(w_pool[i]).astype(BF16)
        pscale = pool_scale[i].reshape(1, -1)
        wup = w_up[i].astype(BF16)
        cw = conv_w[i]
        cb = conv_b[i].reshape(1, -1)
        wd = w_down[i].astype(BF16).reshape(nch, fc, d)

        qx, kx, vtx, ux, svx, xpx = _proj_call(x, m6[0], m6[1], g1n, w_in_b, e_q, e_k, gain,
                                               cos_x, sin_x, proj_x)
        if last:
            kc, vtc = _proj_call(ctx, m6[0], m6[1], g1n, w_in_b, e_q, e_k, gain,
                                 cos_c, sin_c, tile_c, cond_row=b, kv_only=True)
        else:
            qc, kc, vtc, uc, svc, xpc = _proj_call(ctx, m6[0], m6[1], g1n, w_in_b, e_q, e_k, gain,
                                                   cos_c, sin_c, tile_c, cond_row=b)
        score_bound = (1.01 * HEAD_DIM * jnp.max(jnp.abs(gain[0, :N_HEADS * HEAD_DIM]))
                       * jnp.max(jnp.abs(k_gain[i])))
        att_x = _attn_call(qx, [kx, kc], [vtx, vtc], wide_x, score_bound)
        x = _mix_call(att_x, ux, svx, xpx, x, m6[2], ws_b, bias, wp_bd, pscale, icnt_x,
                      w_out_b, wide_x)
        x = _ffn_call(x, m6[3], m6[4], m6[5], g2n, wup, cw, cb, wd, fg,
                      big_x, last)
        if not last:
            att_c = _attn_call(qc, [kc], [vtc], tile_c, score_bound)
            ctx = _mix_call(att_c, uc, svc, xpc, ctx, m6[2], ws_b, bias, wp_bd, pscale, icnt_c,
                            w_out_b, tile_c, cond_row=b)
            ctx = _ffn_call(ctx, m6[3], m6[4], m6[5], g2n, wup, cw, cb, wd, fg,
                            tile_c, False, cond_row=b)
    return x
```
